```python
import math
import jax, jax.numpy as jnp
from jax import lax
import numpy as np

D_MODEL = 1024
BATCH = 4
SEQ = 4096
DEPTH = 2
DEC_BATCH = 16
DEC_SEQ = 4096
PAST_LEN = 128

POOL_WIDTH = 256
POOL_GROUPS = 4
POOL_GROUP_DIM = POOL_WIDTH // POOL_GROUPS
POOL_WINDOWS = (2, 4, 8, 16)
N_HEADS = 8
QK_NOPE = 64
QK_ROPE = 32
QK_DIM = QK_NOPE + QK_ROPE
V_HEAD = 64
Q_LORA = 256
KV_LORA = 128
ROPE_BASE = 10000.0
ATTN_WIDTH = N_HEADS * V_HEAD
Q_BLOCK = 128
CONV_WIDTH = 256
CONV_K = 31
FOURIER_WIDTH = 256
FOURIER_GROUPS = 4
FOURIER_GROUP_DIM = FOURIER_WIDTH // FOURIER_GROUPS
N_BRANCH = 4
NORM_EPS = 1e-6
LN_EPS = 1e-5
IN_SIZES = (POOL_WIDTH, Q_LORA, KV_LORA + QK_ROPE, 2 * CONV_WIDTH, FOURIER_WIDTH,
            POOL_WIDTH, ATTN_WIDTH, CONV_WIDTH, FOURIER_WIDTH, N_BRANCH * D_MODEL)
N_IN = POOL_WIDTH + Q_LORA + KV_LORA + QK_ROPE + 2 * CONV_WIDTH + FOURIER_WIDTH + POOL_WIDTH + ATTN_WIDTH + CONV_WIDTH + FOURIER_WIDTH + N_BRANCH * D_MODEL

kernel_name = "hybrid_parallel_gated_encoder"


def rms_norm(x, g):
    xf = x.astype(jnp.float32)
    y = xf * lax.rsqrt(jnp.mean(xf * xf, axis=-1, keepdims=True) + NORM_EPS)
    return (y * g.astype(jnp.float32)).astype(x.dtype)


def layer_norm(x, g, b):
    xf = x.astype(jnp.float32)
    mu = jnp.mean(xf, axis=-1, keepdims=True)
    var = jnp.mean(jnp.square(xf - mu), axis=-1, keepdims=True)
    y = (xf - mu) * lax.rsqrt(var + LN_EPS)
    return (y * g.astype(jnp.float32) + b.astype(jnp.float32)).astype(x.dtype)


def pool_mixer(u, pool_w, pool_scale):
    B, S, C = u.shape
    uf = u.astype(jnp.float32)
    cs = jnp.concatenate([jnp.zeros((B, 1, C), jnp.float32), jnp.cumsum(uf, axis=1)], axis=1)
    t = jnp.arange(S)
    outs = []
    for g, w in enumerate(POOL_WINDOWS):
        sl = slice(g * POOL_GROUP_DIM, (g + 1) * POOL_GROUP_DIM)
        lo = jnp.clip(t - w // 2, 0, S)
        hi = jnp.clip(t + w // 2, 0, S)
        csg = cs[..., sl]
        win_sum = jnp.take(csg, hi, axis=1) - jnp.take(csg, lo, axis=1)
        cnt = (hi - lo).astype(jnp.float32)[None, :, None]
        outs.append(win_sum / cnt - uf[..., sl])
    p = jnp.stack(outs, axis=2).astype(u.dtype)
    p = jnp.einsum('bsgc,gcd->bsgd', p, pool_w).reshape(B, S, C)
    return p * pool_scale


def rope_tables(S):
    inv_freq = 1.0 / (ROPE_BASE ** (jnp.arange(0, QK_ROPE, 2, dtype=jnp.float32) / QK_ROPE))
    ang = jnp.arange(S, dtype=jnp.float32)[:, None] * inv_freq[None, :]
    ang = jnp.concatenate([ang, ang], axis=-1)
    return jnp.cos(ang)[None, :, None, :], jnp.sin(ang)[None, :, None, :]


def apply_rope(x, cos, sin):
    xf = x.astype(jnp.float32)
    x1, x2 = xf[..., : QK_ROPE // 2], xf[..., QK_ROPE // 2:]
    rot = jnp.concatenate([-x2, x1], axis=-1)
    return (xf * cos + rot * sin).astype(x.dtype)


def mla(c_q, kv_a, q_norm_g, w_uq, kv_norm_g, w_ukv):
    B, S, _ = c_q.shape
    q = (rms_norm(c_q, q_norm_g) @ w_uq).reshape(B, S, N_HEADS, QK_DIM)
    q_nope, q_rope = q[..., :QK_NOPE], q[..., QK_NOPE:]
    c_kv, k_rope = kv_a[..., :KV_LORA], kv_a[..., KV_LORA:]
    kv = (rms_norm(c_kv, kv_norm_g) @ w_ukv).reshape(B, S, N_HEADS, QK_NOPE + V_HEAD)
    k_nope, v = kv[..., :QK_NOPE], kv[..., QK_NOPE:]
    cos, sin = rope_tables(S)
    q_rope = apply_rope(q_rope, cos, sin)
    k_rope = apply_rope(k_rope[:, :, None, :], cos, sin)
    q = jnp.concatenate([q_nope, q_rope], axis=-1)
    k = jnp.concatenate([k_nope, jnp.broadcast_to(k_rope, (B, S, N_HEADS, QK_ROPE))], axis=-1)
    scale = QK_DIM ** -0.5
    nb = S // Q_BLOCK
    qb = q.reshape(B, nb, Q_BLOCK, N_HEADS, QK_DIM).transpose(1, 0, 2, 3, 4)

    def attend(qblk):
        s = jnp.einsum('bqhd,bkhd->bhqk', qblk, k, preferred_element_type=jnp.float32) * scale
        p = jax.nn.softmax(s, axis=-1).astype(v.dtype)
        return jnp.einsum('bhqk,bkhd->bqhd', p, v)

    o = lax.map(attend, qb)
    return o.transpose(1, 0, 2, 3, 4).reshape(B, S, ATTN_WIDTH)


def conv_module(u, conv_w, conv_b, ln_g, ln_b, pw_w, pw_b):
    a, b = u[..., :CONV_WIDTH], u[..., CONV_WIDTH:]
    a = a * jax.nn.sigmoid(b)
    y = lax.conv_general_dilated(a, conv_w[:, None, :].astype(a.dtype), window_strides=(1,),
                                 padding=[(CONV_K // 2, CONV_K // 2)],
                                 dimension_numbers=('NWC', 'WIO', 'NWC'),
                                 feature_group_count=CONV_WIDTH) + conv_b
    y = jax.nn.silu(layer_norm(y, ln_g, ln_b))
    return y @ pw_w + pw_b


def fourier_mixer(u, fourier_w):
    B, S, _ = u.shape
    ug = u.astype(jnp.float32).reshape(B, S, FOURIER_GROUPS, FOURIER_GROUP_DIM)
    f = jnp.fft.fft2(ug, axes=(1, 3), norm='ortho').real
    f = f.reshape(B, S, FOURIER_WIDTH).astype(u.dtype)
    return f @ fourier_w


def encoder_layer(x, pre_g, post_g, w_in, gate_b, q_norm_g, w_uq, kv_norm_g, w_ukv,
                  pool_w, pool_scale, conv_w, conv_b, conv_ln_g, conv_ln_b, conv_pw_w, conv_pw_b,
                  fourier_w, w_up_pool, w_up_attn, w_up_conv, w_up_fourier, w_out):
    B, S, D = x.shape
    h = rms_norm(x, pre_g)
    z = h @ w_in
    points = []
    acc = 0
    for sz in IN_SIZES[:-1]:
        acc += sz
        points.append(acc)
    (u_pool, c_q, kv_a, u_conv, u_four,
     g_pool, g_attn, g_conv, g_four, z_merge) = jnp.split(z, points, axis=-1)
    y_pool = pool_mixer(u_pool, pool_w, pool_scale) * jax.nn.silu(g_pool)
    y_attn = mla(c_q, kv_a, q_norm_g, w_uq, kv_norm_g, w_ukv) * jax.nn.silu(g_attn)
    y_conv = conv_module(u_conv, conv_w, conv_b, conv_ln_g, conv_ln_b, conv_pw_w, conv_pw_b) * jax.nn.silu(g_conv)
    y_four = fourier_mixer(u_four, fourier_w) * jax.nn.silu(g_four)
    gates = jax.nn.sigmoid((z_merge + gate_b).reshape(B, S, N_BRANCH, D))
    m = (gates[:, :, 0] * (y_pool @ w_up_pool)
         + gates[:, :, 1] * (y_attn @ w_up_attn)
         + gates[:, :, 2] * (y_conv @ w_up_conv)
         + gates[:, :, 3] * (y_four @ w_up_fourier))
    out = m @ w_out
    return x + rms_norm(out, post_g)


def setup_inputs(seed: int = 0) -> dict:
    key = jax.random.key(seed)
    ks = jax.random.split(key, 32)
    L, D = DEPTH, D_MODEL
    f32 = jnp.float32

    def nrm(k, shape, fan_in):
        return jax.random.normal(k, shape, f32) * (fan_in ** -0.5)

    def gain(k, shape):
        return 1.0 + 0.02 * jax.random.normal(k, shape, f32)

    def small(k, shape):
        return 0.01 * jax.random.normal(k, shape, f32)

    return {
        "x_prompt": jax.random.normal(ks[0], (BATCH, SEQ, D), f32),
        "x_sample": jax.random.normal(ks[1], (DEC_BATCH, DEC_SEQ, D), f32),
        "pre_norm_g": gain(ks[2], (L, D)),
        "post_norm_g": gain(ks[3], (L, D)),
        "w_in": nrm(ks[4], (L, D, N_IN), D),
        "gate_b": small(ks[5], (L, N_BRANCH * D)),
        "q_norm_g": gain(ks[6], (L, Q_LORA)),
        "w_uq": nrm(ks[7], (L, Q_LORA, N_HEADS * QK_DIM), Q_LORA),
        "kv_norm_g": gain(ks[8], (L, KV_LORA)),
        "w_ukv": nrm(ks[9], (L, KV_LORA, N_HEADS * (QK_NOPE + V_HEAD)), KV_LORA),
        "pool_w": nrm(ks[10], (L, POOL_GROUPS, POOL_GROUP_DIM, POOL_GROUP_DIM), POOL_GROUP_DIM),
        "pool_scale": 1.0 + 0.1 * jax.random.normal(ks[11], (L, POOL_WIDTH), f32),
        "conv_w": nrm(ks[12], (L, CONV_K, CONV_WIDTH), CONV_K),
        "conv_b": small(ks[13], (L, CONV_WIDTH)),
        "conv_ln_g": gain(ks[14], (L, CONV_WIDTH)),
        "conv_ln_b": small(ks[15], (L, CONV_WIDTH)),
        "conv_pw_w": nrm(ks[16], (L, CONV_WIDTH, CONV_WIDTH), CONV_WIDTH),
        "conv_pw_b": small(ks[17], (L, CONV_WIDTH)),
        "fourier_w": nrm(ks[18], (L, FOURIER_WIDTH, FOURIER_WIDTH), FOURIER_WIDTH),
        "w_up_pool": nrm(ks[19], (L, POOL_WIDTH, D), POOL_WIDTH),
        "w_up_attn": nrm(ks[20], (L, ATTN_WIDTH, D), ATTN_WIDTH),
        "w_up_conv": nrm(ks[21], (L, CONV_WIDTH, D), CONV_WIDTH),
        "w_up_fourier": nrm(ks[22], (L, FOURIER_WIDTH, D), FOURIER_WIDTH),
        "w_out": nrm(ks[23], (L, D, D), D),
    }


def reference(x_prompt, x_sample, pre_norm_g, post_norm_g, w_in, gate_b, q_norm_g, w_uq, kv_norm_g, w_ukv,
              pool_w, pool_scale, conv_w, conv_b, conv_ln_g, conv_ln_b, conv_pw_w, conv_pw_b,
              fourier_w, w_up_pool, w_up_attn, w_up_conv, w_up_fourier, w_out):
    def trunk(x):
        for l in range(DEPTH):
            x = encoder_layer(x, pre_norm_g[l], post_norm_g[l], w_in[l], gate_b[l], q_norm_g[l], w_uq[l],
                              kv_norm_g[l], w_ukv[l], pool_w[l], pool_scale[l], conv_w[l], conv_b[l],
                              conv_ln_g[l], conv_ln_b[l], conv_pw_w[l], conv_pw_b[l], fourier_w[l],
                              w_up_pool[l], w_up_attn[l], w_up_conv[l], w_up_fourier[l], w_out[l])
        return x

    y_prompt = trunk(x_prompt)
    y_sample = trunk(x_sample)
    return (y_prompt, y_sample)
```

```python
import functools
import math

import jax
import jax.numpy as jnp
from jax import lax
from jax.experimental import pallas as pl
from jax.experimental.pallas import tpu as pltpu

F32 = jnp.float32
BF16 = jnp.bfloat16

D_MODEL = 1024
POOL_WIDTH = 256
POOL_GROUPS = 4
POOL_GROUP_DIM = POOL_WIDTH // POOL_GROUPS
POOL_WINDOWS = (2, 4, 8, 16)
N_HEADS = 8
QK_NOPE = 64
QK_ROPE = 32
QK_DIM = QK_NOPE + QK_ROPE
V_HEAD = 64
Q_LORA = 256
KV_LORA = 128
ROPE_BASE = 10000.0
ATTN_WIDTH = N_HEADS * V_HEAD
CONV_WIDTH = 256
CONV_K = 31
FOURIER_WIDTH = 256
FOURIER_GROUPS = 4
FOURIER_GROUP_DIM = FOURIER_WIDTH // FOURIER_GROUPS
N_BRANCH = 4
NORM_EPS = 1e-6
LN_EPS = 1e-5

LANES = 128
SUBLANES = 8
HEAD_PAD = LANES
VMEM_LIMIT_BYTES = 56 * 1024 * 1024

Q_SCALE = (QK_DIM ** -0.5) * math.log2(math.e)

C_UPOOL = 0
C_CQ = C_UPOOL + POOL_WIDTH
C_CKV = C_CQ + Q_LORA
C_KRA = C_CKV + KV_LORA
C_KRB = C_KRA + HEAD_PAD
C_CONVA = C_KRB + HEAD_PAD
C_CONVB = C_CONVA + CONV_WIDTH
C_UFOUR = C_CONVB + CONV_WIDTH
C_GATES = C_UFOUR + FOURIER_WIDTH
GATES_WIDTH = POOL_WIDTH + ATTN_WIDTH + CONV_WIDTH + FOURIER_WIDTH
C_END = C_GATES + GATES_WIDTH


def _params(*sem):
    return pltpu.CompilerParams(dimension_semantics=sem, vmem_limit_bytes=VMEM_LIMIT_BYTES)


def _dot(a, b):
    return jnp.dot(a, b, preferred_element_type=F32)


def _rms(x, g):
    return x * lax.rsqrt(jnp.mean(x * x, axis=-1, keepdims=True) + NORM_EPS) * g


def _sigmoid(x):
    return 1.0 / (1.0 + jnp.exp(-x))


def _in_proj_kernel(x_ref, pre_g_ref, w_ref, qg_ref, wq_ref, kvg_ref, wk_ref, wv_ref, cos_ref, sin_ref,
                    upool_ref, q_ref, k_ref, v_ref, aconv_ref, ufour_ref,
                    gpool_ref, gattn_ref, gconv_ref, gfour_ref):
    h = _rms(x_ref[0], pre_g_ref[...]).astype(BF16)

    def proj(lo, hi):
        return _dot(h, w_ref[:, lo:hi])

    upool_ref[0] = proj(C_UPOOL, C_CQ)

    cos = cos_ref[...]
    sin = sin_ref[...]

    cq = _rms(proj(C_CQ, C_CKV), qg_ref[...]).astype(BF16)
    width = N_HEADS * HEAD_PAD
    qa = _dot(cq, wq_ref[:, :width])
    qb = _dot(cq, wq_ref[:, width:])
    for hd in range(N_HEADS):
        sl = slice(hd * HEAD_PAD, (hd + 1) * HEAD_PAD)
        q_ref[0, :, sl] = ((qa[:, sl] * cos + qb[:, sl] * sin) * Q_SCALE).astype(BF16)

    ckv = _rms(proj(C_CKV, C_KRA), kvg_ref[...]).astype(BF16)
    kn = _dot(ckv, wk_ref[...])
    kr = proj(C_KRA, C_KRB) * cos + proj(C_KRB, C_CONVA) * sin
    for hd in range(N_HEADS):
        sl = slice(hd * HEAD_PAD, (hd + 1) * HEAD_PAD)
        k_ref[0, :, sl] = (kn[:, sl] + kr).astype(BF16)
    v_ref[0] = _dot(ckv, wv_ref[...]).astype(BF16)

    aconv_ref[0] = proj(C_CONVA, C_CONVB) * _sigmoid(proj(C_CONVB, C_UFOUR))
    ufour_ref[0] = proj(C_UFOUR, C_GATES).astype(BF16)

    def gate(lo, hi):
        z = proj(C_GATES + lo, C_GATES + hi)
        return (z * _sigmoid(z)).astype(BF16)

    o1 = POOL_WIDTH
    o2 = o1 + ATTN_WIDTH
    o3 = o2 + CONV_WIDTH
    gpool_ref[0] = gate(0, o1)
    gattn_ref[0] = gate(o1, o2)
    gconv_ref[0] = gate(o2, o3)
    gfour_ref[0] = gate(o3, GATES_WIDTH)


def _in_proj(x, pre_g, w_a, qg, wq, kvg, wk, wv, cos_t, sin_t, *, ts, nb):
    B, S, D = x.shape
    grid = (B, S // ts)
    tok = lambda w: pl.BlockSpec((1, ts, w), lambda b, s: (b, s, 0))
    grp = lambda w: pl.BlockSpec((1, ts, w), lambda b, s: (b // nb, s, b % nb))
    full = lambda a: pl.BlockSpec(a.shape, lambda b, s: (0,) * a.ndim)
    rope = pl.BlockSpec((ts, HEAD_PAD), lambda b, s: (s, 0))
    sds = jax.ShapeDtypeStruct
    out_shape = (
        sds((B, S, POOL_WIDTH), F32),
        sds((B, S, N_HEADS * HEAD_PAD), BF16),
        sds((B, S, N_HEADS * HEAD_PAD), BF16),
        sds((B, S, ATTN_WIDTH), BF16),
        sds((B, S, CONV_WIDTH), F32),
        sds((B // nb, S, nb * FOURIER_WIDTH), BF16),
        sds((B, S, POOL_WIDTH), BF16),
        sds((B, S, ATTN_WIDTH), BF16),
        sds((B, S, CONV_WIDTH), BF16),
        sds((B // nb, S, nb * FOURIER_WIDTH), BF16),
    )
    out_specs = (tok(POOL_WIDTH), tok(N_HEADS * HEAD_PAD), tok(N_HEADS * HEAD_PAD), tok(ATTN_WIDTH),
                 tok(CONV_WIDTH), grp(FOURIER_WIDTH), tok(POOL_WIDTH), tok(ATTN_WIDTH), tok(CONV_WIDTH),
                 grp(FOURIER_WIDTH))
    return pl.pallas_call(
        _in_proj_kernel, grid=grid,
        in_specs=[tok(D), full(pre_g), full(w_a), full(qg), full(wq), full(kvg), full(wk), full(wv), rope, rope],
        out_specs=out_specs, out_shape=out_shape,
        compiler_params=_params("parallel", "parallel"), name="in_proj",
    )(x, pre_g, w_a, qg, wq, kvg, wk, wv, cos_t, sin_t)


POOL_HALO = max(POOL_WINDOWS) // 2
POOL_CHUNK = 256


def _pool_kernel(u_ref, g_ref, w_ref, scale_ref, y_ref, pad_ref, *, S):
    zeros = jnp.zeros((POOL_HALO, POOL_WIDTH), F32)
    pad_ref[0:POOL_HALO, :] = zeros
    pad_ref[POOL_HALO + S:POOL_HALO + S + POOL_HALO, :] = zeros
    pad_ref[POOL_HALO:POOL_HALO + S, :] = u_ref[0]

    ch = min(POOL_CHUNK, S)
    lane_group = lax.broadcasted_iota(jnp.int32, (ch, POOL_WIDTH), 1) // POOL_GROUP_DIM
    half = jnp.left_shift(1, lane_group)
    for c in range(S // ch):
        base = c * ch

        def rows(off):
            return pad_ref[POOL_HALO + base + off:POOL_HALO + base + off + ch, :]

        u = rows(0)
        acc = u + rows(-1)
        sums = [acc]
        for w_prev, w_next in zip(POOL_WINDOWS[:-1], POOL_WINDOWS[1:]):
            for off in list(range(-w_next // 2, -w_prev // 2)) + list(range(w_prev // 2, w_next // 2)):
                acc = acc + rows(off)
            sums.append(acc)
        win = jnp.where(lane_group == 0, sums[0],
                        jnp.where(lane_group == 1, sums[1], jnp.where(lane_group == 2, sums[2], sums[3])))
        t = lax.broadcasted_iota(jnp.int32, (ch, POOL_WIDTH), 0) + base
        cnt = (jnp.minimum(t + half, S) - jnp.maximum(t - half, 0)).astype(F32)
        p = (win / cnt - u).astype(BF16)
        y = _dot(p, w_ref[...]) * scale_ref[...] * g_ref[0, base:base + ch, :].astype(F32)
        y_ref[0, base:base + ch, :] = y.astype(BF16)


def _pool(u, g, w_bd, scale):
    B, S, W = u.shape
    seq = pl.BlockSpec((1, S, W), lambda b: (b, 0, 0))
    full = lambda a: pl.BlockSpec(a.shape, lambda b: (0,) * a.ndim)
    return pl.pallas_call(
        functools.partial(_pool_kernel, S=S), grid=(B,),
        in_specs=[seq, seq, full(w_bd), full(scale)], out_specs=seq,
        out_shape=jax.ShapeDtypeStruct((B, S, W), BF16),
        scratch_shapes=[pltpu.VMEM((S + 2 * POOL_HALO, W), F32)],
        compiler_params=_params("parallel"), name="pool",
    )(u, g, w_bd, scale)


CONV_HALO = 16
CONV_CHUNK = 128


def _conv_kernel(a_ref, g_ref, cw_ref, cb_ref, lng_ref, lnb_ref, pw_ref, pwb_ref, y_ref, pad_ref, *, S):
    zeros = jnp.zeros((CONV_HALO, CONV_WIDTH), F32)
    pad_ref[0:CONV_HALO, :] = zeros
    pad_ref[CONV_HALO + S:CONV_HALO + S + CONV_HALO, :] = zeros
    pad_ref[CONV_HALO:CONV_HALO + S, :] = a_ref[0]

    ch = min(CONV_CHUNK, S)
    first = CONV_HALO - CONV_K // 2
    for c in range(S // ch):
        base = c * ch
        acc = jnp.zeros((ch, CONV_WIDTH), F32) + cb_ref[...]
        for k in range(CONV_K):
            acc = acc + pad_ref[first + base + k:first + base + k + ch, :] * cw_ref[k:k + 1, :]
        mu = jnp.mean(acc, axis=-1, keepdims=True)
        d = acc - mu
        var = jnp.mean(d * d, axis=-1, keepdims=True)
        yn = d * lax.rsqrt(var + LN_EPS) * lng_ref[...] + lnb_ref[...]
        act = (yn * _sigmoid(yn)).astype(BF16)
        y = (_dot(act, pw_ref[...]) + pwb_ref[...]) * g_ref[0, base:base + ch, :].astype(F32)
        y_ref[0, base:base + ch, :] = y.astype(BF16)


def _conv(a, g, cw, cb, lng, lnb, pw, pwb):
    B, S, W = a.shape
    seq = pl.BlockSpec((1, S, W), lambda b: (b, 0, 0))
    full = lambda t: pl.BlockSpec(t.shape, lambda b: (0,) * t.ndim)
    return pl.pallas_call(
        functools.partial(_conv_kernel, S=S), grid=(B,),
        in_specs=[seq, seq, full(cw), full(cb), full(lng), full(lnb), full(pw), full(pwb)], out_specs=seq,
        out_shape=jax.ShapeDtypeStruct((B, S, W), BF16),
        scratch_shapes=[pltpu.VMEM((S + 2 * CONV_HALO, W), F32)],
        compiler_params=_params("parallel"), name="conv",
    )(a, g, cw, cb, lng, lnb, pw, pwb)


def _fourier_kernel(c_ref, s_ref, u_ref, cb_ref, sb_ref, fw_ref, g_ref, y_ref, *, nb):
    u = u_ref[0]
    a = _dot(c_ref[...], u).astype(BF16)
    b = _dot(s_ref[...], u).astype(BF16)
    for n in range(nb):
        sl = slice(n * FOURIER_WIDTH, (n + 1) * FOURIER_WIDTH)
        f = _dot(a[:, sl], cb_ref[...]) - _dot(b[:, sl], sb_ref[...])
        y = _dot(f.astype(BF16), fw_ref[...]) * g_ref[0, :, sl].astype(F32)
        y_ref[0, :, sl] = y.astype(BF16)


def _fourier(cmat, smat, u, cb, sb, fw, g, *, tk, nb):
    G, S, W = u.shape
    rows = pl.BlockSpec((tk, S), lambda i, j: (j, 0))
    full = lambda t: pl.BlockSpec(t.shape, lambda i, j: (0,) * t.ndim)
    tile = pl.BlockSpec((1, tk, W), lambda i, j: (i, j, 0))
    return pl.pallas_call(
        functools.partial(_fourier_kernel, nb=nb), grid=(G, S // tk),
        in_specs=[rows, rows, pl.BlockSpec((1, S, W), lambda i, j: (i, 0, 0)), full(cb), full(sb), full(fw), tile],
        out_specs=tile, out_shape=jax.ShapeDtypeStruct((G, S, W), BF16),
        compiler_params=_params("parallel", "parallel"), name="fourier",
    )(cmat, smat, u, cb, sb, fw, g)


def _attn_kernel(q_ref, k_ref, v_ref, g_ref, o_ref):
    v = v_ref[0]
    outs = []
    for hh in range(2):
        sl = slice(hh * HEAD_PAD, (hh + 1) * HEAD_PAD)
        s = lax.dot_general(q_ref[0, :, sl], k_ref[0, :, sl], (((1,), (1,)), ((), ())),
                            preferred_element_type=F32)
        m = jnp.max(s, axis=-1, keepdims=True)
        p = jnp.exp2(s - m)
        l = jnp.sum(p, axis=-1, keepdims=True)
        outs.append(_dot(p.astype(BF16), v) / l)
    lane = lax.broadcasted_iota(jnp.int32, outs[0].shape, 1)
    o = jnp.where(lane < V_HEAD, outs[0], outs[1])
    o_ref[0] = (o * g_ref[0].astype(F32)).astype(BF16)


def _attention(q, k, v, g, *, tq):
    B, S, _ = q.shape
    pair = 2 * HEAD_PAD
    grid = (B, N_HEADS // 2, S // tq)
    return pl.pallas_call(
        _attn_kernel, grid=grid,
        in_specs=[pl.BlockSpec((1, tq, pair), lambda b, h, i: (b, i, h)),
                  pl.BlockSpec((1, S, pair), lambda b, h, i: (b, 0, h)),
                  pl.BlockSpec((1, S, 2 * V_HEAD), lambda b, h, i: (b, 0, h)),
                  pl.BlockSpec((1, tq, 2 * V_HEAD), lambda b, h, i: (b, i, h))],
        out_specs=pl.BlockSpec((1, tq, 2 * V_HEAD), lambda b, h, i: (b, i, h)),
        out_shape=jax.ShapeDtypeStruct((B, S, ATTN_WIDTH), BF16),
        compiler_params=_params("parallel", "parallel", "parallel"), name="attention",
    )(q, k, v, g)


def _merge_kernel(x_ref, pre_g_ref, wm_ref, gb_ref, yp_ref, ya_ref, yc_ref, yf_ref,
                  wup_ref, wua_ref, wuc_ref, wuf_ref, wo_ref, post_g_ref, o_ref):
    x = x_ref[0]
    h = _rms(x, pre_g_ref[...]).astype(BF16)
    m = None
    branches = ((yp_ref, wup_ref), (ya_ref, wua_ref), (yc_ref, wuc_ref), (yf_ref, wuf_ref))
    for i, (y_ref, wu_ref) in enumerate(branches):
        sl = slice(i * D_MODEL, (i + 1) * D_MODEL)
        gate = _sigmoid(_dot(h, wm_ref[:, sl]) + gb_ref[:, sl])
        term = gate * _dot(y_ref[0], wu_ref[...])
        m = term if m is None else m + term
    out = _dot(m.astype(BF16), wo_ref[...])
    o_ref[0] = x + _rms(out, post_g_ref[...])


def _merge(x, pre_g, wm, gb, yp, ya, yc, yf, wup, wua, wuc, wuf, wo, post_g, *, ts, nb):
    B, S, D = x.shape
    tok = lambda w: pl.BlockSpec((1, ts, w), lambda b, s: (b, s, 0))
    grp = lambda w: pl.BlockSpec((1, ts, w), lambda b, s: (b // nb, s, b % nb))
    full = lambda a: pl.BlockSpec(a.shape, lambda b, s: (0,) * a.ndim)
    return pl.pallas_call(
        _merge_kernel, grid=(B, S // ts),
        in_specs=[tok(D), full(pre_g), full(wm), full(gb), tok(POOL_WIDTH), tok(ATTN_WIDTH), tok(CONV_WIDTH),
                  grp(FOURIER_WIDTH), full(wup), full(wua), full(wuc), full(wuf), full(wo), full(post_g)],
        out_specs=tok(D), out_shape=jax.ShapeDtypeStruct((B, S, D), F32),
        compiler_params=_params("parallel", "parallel"), name="merge",
    )(x, pre_g, wm, gb, yp, ya, yc, yf, wup, wua, wuc, wuf, wo, post_g)


def _rot_half_cols(w):
    half = QK_ROPE // 2
    return jnp.concatenate([-w[..., half:], w[..., :half]], axis=-1)


def _block_diag(blocks):
    g, n, _ = blocks.shape
    eye = jnp.eye(g, dtype=blocks.dtype)
    return jnp.einsum('gh,gcd->gchd', eye, blocks).reshape(g * n, g * n)


def _prep_layer(w_in, w_uq, w_ukv, pool_w):
    D = w_in.shape[0]
    offs = [0]
    for sz in (POOL_WIDTH, Q_LORA, KV_LORA, QK_ROPE, CONV_WIDTH, CONV_WIDTH, FOURIER_WIDTH, GATES_WIDTH,
               N_BRANCH * D_MODEL):
        offs.append(offs[-1] + sz)
    cols = [w_in[:, offs[i]:offs[i + 1]] for i in range(len(offs) - 1)]
    w_upool, w_cq, w_ckv, w_kr, w_ca, w_cb, w_uf, w_g, w_m = cols
    zl = jnp.zeros((D, QK_NOPE), F32)
    zr = jnp.zeros((D, HEAD_PAD - QK_DIM), F32)
    w_kra = jnp.concatenate([zl, w_kr, zr], axis=1)
    w_krb = jnp.concatenate([zl, _rot_half_cols(w_kr), zr], axis=1)
    w_a = jnp.concatenate([w_upool, w_cq, w_ckv, w_kra, w_krb, w_ca, w_cb, w_uf, w_g], axis=1).astype(BF16)

    wq = w_uq.reshape(Q_LORA, N_HEADS, QK_DIM)
    nope, ropew = wq[..., :QK_NOPE], wq[..., QK_NOPE:]
    zpad = jnp.zeros((Q_LORA, N_HEADS, HEAD_PAD - QK_DIM), F32)
    wq_a = jnp.concatenate([nope, ropew, zpad], axis=-1).reshape(Q_LORA, N_HEADS * HEAD_PAD)
    wq_b = jnp.concatenate([jnp.zeros_like(nope), _rot_half_cols(ropew), zpad], axis=-1)
    wq_b = wq_b.reshape(Q_LORA, N_HEADS * HEAD_PAD)
    wq_ab = jnp.concatenate([wq_a, wq_b], axis=1).astype(BF16)

    wkv = w_ukv.reshape(KV_LORA, N_HEADS, QK_NOPE + V_HEAD)
    wk = jnp.concatenate([wkv[..., :QK_NOPE], jnp.zeros((KV_LORA, N_HEADS, HEAD_PAD - QK_NOPE), F32)], axis=-1)
    wk = wk.reshape(KV_LORA, N_HEADS * HEAD_PAD).astype(BF16)
    wv = wkv[..., QK_NOPE:].reshape(KV_LORA, ATTN_WIDTH).astype(BF16)
    return w_a, w_m.astype(BF16), wq_ab, wk, wv, _block_diag(pool_w).astype(BF16)


def _rope_tables(S):
    inv_freq = 1.0 / (ROPE_BASE ** (jnp.arange(0, QK_ROPE, 2, dtype=F32) / QK_ROPE))
    ang = jnp.arange(S, dtype=F32)[:, None] * inv_freq[None, :]
    ang = jnp.concatenate([ang, ang], axis=-1)
    pad = jnp.zeros((S, HEAD_PAD - QK_DIM), F32)
    cos_t = jnp.concatenate([jnp.ones((S, QK_NOPE), F32), jnp.cos(ang), pad], axis=-1)
    sin_t = jnp.concatenate([jnp.zeros((S, QK_NOPE), F32), jnp.sin(ang), pad], axis=-1)
    return cos_t, sin_t


def _dft_tables(n):
    idx = jnp.arange(n, dtype=jnp.int32)
    ang = ((idx[:, None] * idx[None, :]) % n).astype(F32) * (2.0 * math.pi / n)
    return jnp.cos(ang), jnp.sin(ang)


def _group_size(B):
    for nb in (4, 2, 1):
        if B % nb == 0:
            return nb


def _tile(S, t):
    return t if S % t == 0 else S


def _layer(x, tabs, pre_g, post_g, w_in, gate_b, q_norm_g, w_uq, kv_norm_g, w_ukv, pool_w, pool_scale,
           conv_w, conv_b, conv_ln_g, conv_ln_b, conv_pw_w, conv_pw_b, fourier_w,
           w_up_pool, w_up_attn, w_up_conv, w_up_fourier, w_out):
    B, S, D = x.shape
    nb = _group_size(B)
    ts, tq, tk = _tile(S, 512), _tile(S, 256), _tile(S, 256)
    cos_t, sin_t, cmat, smat, cb, sb = tabs
    row = lambda a: a.reshape(1, -1)
    w_a, w_m, wq_ab, wk, wv, pool_bd = _prep_layer(w_in, w_uq, w_ukv, pool_w)

    (u_pool, q, k, v, a_conv, u_four, g_pool, g_attn, g_conv, g_four) = _in_proj(
        x, row(pre_g), w_a, row(q_norm_g), wq_ab, row(kv_norm_g), wk, wv, cos_t, sin_t, ts=ts, nb=nb)
    y_pool = _pool(u_pool, g_pool, pool_bd, row(pool_scale))
    y_conv = _conv(a_conv, g_conv, conv_w, row(conv_b), row(conv_ln_g), row(conv_ln_b),
                   conv_pw_w.astype(BF16), row(conv_pw_b))
    y_four = _fourier(cmat, smat, u_four, cb, sb, fourier_w.astype(BF16), g_four, tk=tk, nb=nb)
    y_attn = _attention(q, k, v, g_attn, tq=tq)
    return _merge(x, row(pre_g), w_m, row(gate_b), y_pool, y_attn, y_conv, y_four,
                  w_up_pool.astype(BF16), w_up_attn.astype(BF16), w_up_conv.astype(BF16),
                  w_up_fourier.astype(BF16), w_out.astype(BF16), row(post_g), ts=ts, nb=nb)


def _tables(S):
    cos_t, sin_t = _rope_tables(S)
    cmat, smat = _dft_tables(S)
    c_ch, s_ch = _dft_tables(FOURIER_GROUP_DIM)
    norm = 1.0 / math.sqrt(S * FOURIER_GROUP_DIM)
    eye = jnp.eye(FOURIER_GROUPS, dtype=F32)
    cb = jnp.kron(eye, c_ch * norm).astype(BF16)
    sb = jnp.kron(eye, s_ch * norm).astype(BF16)
    return cos_t, sin_t, cmat.astype(BF16), smat.astype(BF16), cb, sb


def _trunk(x, tabs, layer_weights):
    depth = layer_weights[0].shape[0]
    for l in range(depth):
        x = _layer(x, tabs, *[w[l] for w in layer_weights])
    return x


def kernel(x_prompt, x_sample, pre_norm_g, post_norm_g, w_in, gate_b, q_norm_g, w_uq, kv_norm_g, w_ukv, pool_w, pool_scale, conv_w, conv_b, conv_ln_g, conv_ln_b, conv_pw_w, conv_pw_b, fourier_w, w_up_pool, w_up_attn, w_up_conv, w_up_fourier, w_out):
    weights = (pre_norm_g, post_norm_g, w_in, gate_b, q_norm_g, w_uq, kv_norm_g, w_ukv, pool_w, pool_scale,
               conv_w, conv_b, conv_ln_g, conv_ln_b, conv_pw_w, conv_pw_b, fourier_w,
               w_up_pool, w_up_attn, w_up_conv, w_up_fourier, w_out)
    tabs = _tables(x_prompt.shape[1])
    assert x_sample.shape[1] == x_prompt.shape[1]
    return (_trunk(x_prompt, tabs, weights), _trunk(x_sample, tabs, weights))
```

```python
import functools
import math

import jax
import jax.numpy as jnp
from jax import lax
from jax.experimental import pallas as pl
from jax.experimental.pallas import tpu as pltpu

F32 = jnp.float32
BF16 = jnp.bfloat16

D_MODEL = 1024
POOL_WIDTH = 256
POOL_GROUPS = 4
POOL_GROUP_DIM = POOL_WIDTH // POOL_GROUPS
POOL_WINDOWS = (2, 4, 8, 16)
N_HEADS = 8
QK_NOPE = 64
QK_ROPE = 32
QK_DIM = QK_NOPE + QK_ROPE
V_HEAD = 64
Q_LORA = 256
KV_LORA = 128
ROPE_BASE = 10000.0
ATTN_WIDTH = N_HEADS * V_HEAD
CONV_WIDTH = 256
CONV_K = 31
FOURIER_WIDTH = 256
FOURIER_GROUPS = 4
FOURIER_GROUP_DIM = FOURIER_WIDTH // FOURIER_GROUPS
N_BRANCH = 4
NORM_EPS = 1e-6
LN_EPS = 1e-5

LANES = 128
SUBLANES = 8
HEAD_PAD = LANES
VMEM_LIMIT_BYTES = 56 * 1024 * 1024

Q_SCALE = (QK_DIM ** -0.5) * math.log2(math.e)

C_UPOOL = 0
C_CQ = C_UPOOL + POOL_WIDTH
C_CKV = C_CQ + Q_LORA
C_KRA = C_CKV + KV_LORA
C_KRB = C_KRA + HEAD_PAD
C_CONVA = C_KRB + HEAD_PAD
C_CONVB = C_CONVA + CONV_WIDTH
C_UFOUR = C_CONVB + CONV_WIDTH
C_GATES = C_UFOUR + FOURIER_WIDTH
GATES_WIDTH = POOL_WIDTH + ATTN_WIDTH + CONV_WIDTH + FOURIER_WIDTH
C_END = C_GATES + GATES_WIDTH


def _params(*sem):
    return pltpu.CompilerParams(dimension_semantics=sem, vmem_limit_bytes=VMEM_LIMIT_BYTES)


def _dot(a, b):
    return jnp.dot(a, b, preferred_element_type=F32)


def _rms(x, g):
    return x * lax.rsqrt(jnp.mean(x * x, axis=-1, keepdims=True) + NORM_EPS) * g


def _sigmoid(x):
    return 1.0 / (1.0 + jnp.exp(-x))


def _in_proj_kernel(x_ref, pre_g_ref, w_ref, qg_ref, wq_ref, kvg_ref, wk_ref, wv_ref, cos_ref, sin_ref,
                    upool_ref, q_ref, k_ref, v_ref, aconv_ref, ufour_ref,
                    gpool_ref, gattn_ref, gconv_ref, gfour_ref):
    h = _rms(x_ref[0], pre_g_ref[...]).astype(BF16)

    def proj(lo, hi):
        return _dot(h, w_ref[:, lo:hi])

    upool_ref[0] = proj(C_UPOOL, C_CQ)

    cos = cos_ref[...]
    sin = sin_ref[...]

    cq = _rms(proj(C_CQ, C_CKV), qg_ref[...]).astype(BF16)
    width = N_HEADS * HEAD_PAD
    qa = _dot(cq, wq_ref[:, :width])
    qb = _dot(cq, wq_ref[:, width:])
    for hd in range(N_HEADS):
        sl = slice(hd * HEAD_PAD, (hd + 1) * HEAD_PAD)
        q_ref[0, :, sl] = ((qa[:, sl] * cos + qb[:, sl] * sin) * Q_SCALE).astype(BF16)

    ckv = _rms(proj(C_CKV, C_KRA), kvg_ref[...]).astype(BF16)
    kn = _dot(ckv, wk_ref[...])
    kr = proj(C_KRA, C_KRB) * cos + proj(C_KRB, C_CONVA) * sin
    for hd in range(N_HEADS):
        sl = slice(hd * HEAD_PAD, (hd + 1) * HEAD_PAD)
        k_ref[0, :, sl] = (kn[:, sl] + kr).astype(BF16)
    v_ref[0] = _dot(ckv, wv_ref[...]).astype(BF16)

    aconv_ref[0] = proj(C_CONVA, C_CONVB) * _sigmoid(proj(C_CONVB, C_UFOUR))
    ufour_ref[0] = proj(C_UFOUR, C_GATES).astype(BF16)

    def gate(lo, hi):
        z = proj(C_GATES + lo, C_GATES + hi)
        return (z * _sigmoid(z)).astype(BF16)

    o1 = POOL_WIDTH
    o2 = o1 + ATTN_WIDTH
    o3 = o2 + CONV_WIDTH
    gpool_ref[0] = gate(0, o1)
    gattn_ref[0] = gate(o1, o2)
    gconv_ref[0] = gate(o2, o3)
    gfour_ref[0] = gate(o3, GATES_WIDTH)


def _in_proj(x, pre_g, w_a, qg, wq, kvg, wk, wv, cos_t, sin_t, *, ts, nb):
    B, S, D = x.shape
    grid = (B, S // ts)
    tok = lambda w: pl.BlockSpec((1, ts, w), lambda b, s: (b, s, 0))
    grp = lambda w: pl.BlockSpec((1, ts, w), lambda b, s: (b // nb, s, b % nb))
    full = lambda a: pl.BlockSpec(a.shape, lambda b, s: (0,) * a.ndim)
    rope = pl.BlockSpec((ts, HEAD_PAD), lambda b, s: (s, 0))
    sds = jax.ShapeDtypeStruct
    out_shape = (
        sds((B, S, POOL_WIDTH), F32),
        sds((B, S, N_HEADS * HEAD_PAD), BF16),
        sds((B, S, N_HEADS * HEAD_PAD), BF16),
        sds((B, S, ATTN_WIDTH), BF16),
        sds((B, S, CONV_WIDTH), F32),
        sds((B // nb, S, nb * FOURIER_WIDTH), BF16),
        sds((B, S, POOL_WIDTH), BF16),
        sds((B, S, ATTN_WIDTH), BF16),
        sds((B, S, CONV_WIDTH), BF16),
        sds((B // nb, S, nb * FOURIER_WIDTH), BF16),
    )
    out_specs = (tok(POOL_WIDTH), tok(N_HEADS * HEAD_PAD), tok(N_HEADS * HEAD_PAD), tok(ATTN_WIDTH),
                 tok(CONV_WIDTH), grp(FOURIER_WIDTH), tok(POOL_WIDTH), tok(ATTN_WIDTH), tok(CONV_WIDTH),
                 grp(FOURIER_WIDTH))
    return pl.pallas_call(
        _in_proj_kernel, grid=grid,
        in_specs=[tok(D), full(pre_g), full(w_a), full(qg), full(wq), full(kvg), full(wk), full(wv), rope, rope],
        out_specs=out_specs, out_shape=out_shape,
        compiler_params=_params("parallel", "parallel"), name="in_proj",
    )(x, pre_g, w_a, qg, wq, kvg, wk, wv, cos_t, sin_t)


POOL_HALO = max(POOL_WINDOWS) // 2
POOL_CHUNK = 256


def _pool_kernel(u_ref, g_ref, w_ref, scale_ref, y_ref, pad_ref, *, S):
    zeros = jnp.zeros((POOL_HALO, POOL_WIDTH), F32)
    pad_ref[0:POOL_HALO, :] = zeros
    pad_ref[POOL_HALO + S:POOL_HALO + S + POOL_HALO, :] = zeros
    pad_ref[POOL_HALO:POOL_HALO + S, :] = u_ref[0]

    ch = min(POOL_CHUNK, S)
    lane_group = lax.broadcasted_iota(jnp.int32, (ch, POOL_WIDTH), 1) // POOL_GROUP_DIM
    half = jnp.left_shift(1, lane_group)
    for c in range(S // ch):
        base = c * ch

        def rows(off):
            return pad_ref[POOL_HALO + base + off:POOL_HALO + base + off + ch, :]

        u = rows(0)
        acc = u + rows(-1)
        sums = [acc]
        for w_prev, w_next in zip(POOL_WINDOWS[:-1], POOL_WINDOWS[1:]):
            for off in list(range(-w_next // 2, -w_prev // 2)) + list(range(w_prev // 2, w_next // 2)):
                acc = acc + rows(off)
            sums.append(acc)
        win = jnp.where(lane_group == 0, sums[0],
                        jnp.where(lane_group == 1, sums[1], jnp.where(lane_group == 2, sums[2], sums[3])))
        t = lax.broadcasted_iota(jnp.int32, (ch, POOL_WIDTH), 0) + base
        cnt = (jnp.minimum(t + half, S) - jnp.maximum(t - half, 0)).astype(F32)
        p = (win / cnt - u).astype(BF16)
        y = _dot(p, w_ref[...]) * scale_ref[...] * g_ref[0, base:base + ch, :].astype(F32)
        y_ref[0, base:base + ch, :] = y.astype(BF16)


def _pool(u, g, w_bd, scale):
    B, S, W = u.shape
    seq = pl.BlockSpec((1, S, W), lambda b: (b, 0, 0))
    full = lambda a: pl.BlockSpec(a.shape, lambda b: (0,) * a.ndim)
    return pl.pallas_call(
        functools.partial(_pool_kernel, S=S), grid=(B,),
        in_specs=[seq, seq, full(w_bd), full(scale)], out_specs=seq,
        out_shape=jax.ShapeDtypeStruct((B, S, W), BF16),
        scratch_shapes=[pltpu.VMEM((S + 2 * POOL_HALO, W), F32)],
        compiler_params=_params("parallel"), name="pool",
    )(u, g, w_bd, scale)


CONV_HALO = 16
CONV_CHUNK = 512
CONV_ROWS = 64
CONV_FIRST = CONV_HALO - CONV_K // 2
CONV_SPAN = (CONV_FIRST + CONV_K - 1) // SUBLANES * SUBLANES


def _conv_kernel(a_ref, g_ref, cw_ref, cb_ref, lng_ref, lnb_ref, pw_ref, pwb_ref, y_ref, pad_ref, sh_ref, acc_ref,
                 *, S):
    zeros = jnp.zeros((CONV_HALO, CONV_WIDTH), F32)
    pad_ref[0:CONV_HALO, :] = zeros
    pad_ref[CONV_HALO + S:CONV_HALO + S + CONV_HALO, :] = zeros
    pad_ref[CONV_HALO:CONV_HALO + S, :] = a_ref[0]

    ch = min(CONV_CHUNK, S)
    rb = min(CONV_ROWS, ch)
    for c in range(S // ch):
        base = c * ch
        for r in range(1, SUBLANES):
            sh_ref[r - 1] = pad_ref[base + r:base + r + ch + CONV_SPAN, :]
        for sb in range(ch // rb):
            row = sb * rb
            acc = jnp.zeros((rb, CONV_WIDTH), F32) + cb_ref[...]
            for k in range(CONV_K):
                off = CONV_FIRST + k
                al, r = off // SUBLANES * SUBLANES, off % SUBLANES
                if r == 0:
                    x = pad_ref[base + row + al:base + row + al + rb, :]
                else:
                    x = sh_ref[r - 1, row + al:row + al + rb, :]
                acc = acc + x * cw_ref[k:k + 1, :]
            acc_ref[row:row + rb, :] = acc
        y = acc_ref[...]
        mu = jnp.mean(y, axis=-1, keepdims=True)
        d = y - mu
        var = jnp.mean(d * d, axis=-1, keepdims=True)
        yn = d * lax.rsqrt(var + LN_EPS) * lng_ref[...] + lnb_ref[...]
        act = (yn * _sigmoid(yn)).astype(BF16)
        out = (_dot(act, pw_ref[...]) + pwb_ref[...]) * g_ref[0, base:base + ch, :].astype(F32)
        y_ref[0, base:base + ch, :] = out.astype(BF16)


def _conv(a, g, cw, cb, lng, lnb, pw, pwb):
    B, S, W = a.shape
    ch = min(CONV_CHUNK, S)
    seq = pl.BlockSpec((1, S, W), lambda b: (b, 0, 0))
    full = lambda t: pl.BlockSpec(t.shape, lambda b: (0,) * t.ndim)
    return pl.pallas_call(
        functools.partial(_conv_kernel, S=S), grid=(B,),
        in_specs=[seq, seq, full(cw), full(cb), full(lng), full(lnb), full(pw), full(pwb)], out_specs=seq,
        out_shape=jax.ShapeDtypeStruct((B, S, W), BF16),
        scratch_shapes=[pltpu.VMEM((S + 2 * CONV_HALO, W), F32),
                        pltpu.VMEM((SUBLANES - 1, ch + CONV_SPAN, W), F32),
                        pltpu.VMEM((ch, W), F32)],
        compiler_params=_params("parallel"), name="conv",
    )(a, g, cw, cb, lng, lnb, pw, pwb)


def _fourier_kernel(cbase_ref, sbase_ref, crow_ref, srow_ref, u_ref, cb_ref, sb_ref, fw_ref, g_ref, y_ref, *, nb):
    c0, s0 = crow_ref[0], srow_ref[0]
    cbase, sbase = cbase_ref[...], sbase_ref[...]
    cmat = (cbase * c0 - sbase * s0).astype(BF16)
    smat = (sbase * c0 + cbase * s0).astype(BF16)
    u = u_ref[0]
    a = _dot(cmat, u).astype(BF16)
    b = _dot(smat, u).astype(BF16)
    for n in range(nb):
        sl = slice(n * FOURIER_WIDTH, (n + 1) * FOURIER_WIDTH)
        f = _dot(a[:, sl], cb_ref[...]) - _dot(b[:, sl], sb_ref[...])
        y = _dot(f.astype(BF16), fw_ref[...]) * g_ref[0, :, sl].astype(F32)
        y_ref[0, :, sl] = y.astype(BF16)


def _fourier(cbase, sbase, crow, srow, u, cb, sb, fw, g, *, tk, nb):
    G, S, W = u.shape
    full = lambda t: pl.BlockSpec(t.shape, lambda i, j: (0,) * t.ndim)
    row = pl.BlockSpec((1, 1, S), lambda i, j: (j, 0, 0))
    tile = pl.BlockSpec((1, tk, W), lambda i, j: (i, j, 0))
    return pl.pallas_call(
        functools.partial(_fourier_kernel, nb=nb), grid=(G, S // tk),
        in_specs=[full(cbase), full(sbase), row, row, pl.BlockSpec((1, S, W), lambda i, j: (i, 0, 0)),
                  full(cb), full(sb), full(fw), tile],
        out_specs=tile, out_shape=jax.ShapeDtypeStruct((G, S, W), BF16),
        compiler_params=_params("parallel", "parallel"), name="fourier",
    )(cbase, sbase, crow, srow, u, cb, sb, fw, g)


def _attn_kernel(q_ref, k_ref, v_ref, g_ref, o_ref):
    def scores(h):
        sl = slice(h * HEAD_PAD, (h + 1) * HEAD_PAD)
        return lax.dot_general(q_ref[0, :, sl], k_ref[0, :, sl], (((1,), (1,)), ((), ())),
                               preferred_element_type=F32)

    def finish(h, s):
        v = v_ref[0, :, h // 2 * LANES:(h // 2 + 1) * LANES]
        m = jnp.max(s, axis=-1, keepdims=True)
        p = jnp.exp2(s - m)
        l = jnp.sum(p, axis=-1, keepdims=True)
        return _dot(p.astype(BF16), v) / l

    outs = []
    s_prev = scores(0)
    for h in range(1, N_HEADS):
        s_next = scores(h)
        outs.append(finish(h - 1, s_prev))
        s_prev = s_next
    outs.append(finish(N_HEADS - 1, s_prev))
    lane = lax.broadcasted_iota(jnp.int32, outs[0].shape, 1)
    for pr in range(N_HEADS // 2):
        o = jnp.where(lane < V_HEAD, outs[2 * pr], outs[2 * pr + 1])
        sl = slice(pr * LANES, (pr + 1) * LANES)
        o_ref[0, :, sl] = (o * g_ref[0, :, sl].astype(F32)).astype(BF16)


def _attention(q, k, v, g, *, tq):
    B, S, _ = q.shape
    qblk = pl.BlockSpec((1, tq, N_HEADS * HEAD_PAD), lambda b, i: (b, i, 0))
    oblk = pl.BlockSpec((1, tq, ATTN_WIDTH), lambda b, i: (b, i, 0))
    return pl.pallas_call(
        _attn_kernel, grid=(B, S // tq),
        in_specs=[qblk,
                  pl.BlockSpec((1, S, N_HEADS * HEAD_PAD), lambda b, i: (b, 0, 0)),
                  pl.BlockSpec((1, S, ATTN_WIDTH), lambda b, i: (b, 0, 0)),
                  oblk],
        out_specs=oblk,
        out_shape=jax.ShapeDtypeStruct((B, S, ATTN_WIDTH), BF16),
        compiler_params=_params("parallel", "parallel"), name="attention",
    )(q, k, v, g)


def _merge_kernel(x_ref, pre_g_ref, wm_ref, gb_ref, yp_ref, ya_ref, yc_ref, yf_ref,
                  wup_ref, wua_ref, wuc_ref, wuf_ref, wo_ref, post_g_ref, o_ref):
    x = x_ref[0]
    h = _rms(x, pre_g_ref[...]).astype(BF16)
    m = None
    branches = ((yp_ref, wup_ref), (ya_ref, wua_ref), (yc_ref, wuc_ref), (yf_ref, wuf_ref))
    for i, (y_ref, wu_ref) in enumerate(branches):
        sl = slice(i * D_MODEL, (i + 1) * D_MODEL)
        gate = _sigmoid(_dot(h, wm_ref[:, sl]) + gb_ref[:, sl])
        term = gate * _dot(y_ref[0], wu_ref[...])
        m = term if m is None else m + term
    out = _dot(m.astype(BF16), wo_ref[...])
    o_ref[0] = x + _rms(out, post_g_ref[...])


def _merge(x, pre_g, wm, gb, yp, ya, yc, yf, wup, wua, wuc, wuf, wo, post_g, *, ts, nb):
    B, S, D = x.shape
    tok = lambda w: pl.BlockSpec((1, ts, w), lambda b, s: (b, s, 0))
    grp = lambda w: pl.BlockSpec((1, ts, w), lambda b, s: (b // nb, s, b % nb))
    full = lambda a: pl.BlockSpec(a.shape, lambda b, s: (0,) * a.ndim)
    return pl.pallas_call(
        _merge_kernel, grid=(B, S // ts),
        in_specs=[tok(D), full(pre_g), full(wm), full(gb), tok(POOL_WIDTH), tok(ATTN_WIDTH), tok(CONV_WIDTH),
                  grp(FOURIER_WIDTH), full(wup), full(wua), full(wuc), full(wuf), full(wo), full(post_g)],
        out_specs=tok(D), out_shape=jax.ShapeDtypeStruct((B, S, D), F32),
        compiler_params=_params("parallel", "parallel"), name="merge",
    )(x, pre_g, wm, gb, yp, ya, yc, yf, wup, wua, wuc, wuf, wo, post_g)


def _rot_half_cols(w):
    half = QK_ROPE // 2
    return jnp.concatenate([-w[..., half:], w[..., :half]], axis=-1)


def _block_diag(blocks):
    g, n, _ = blocks.shape
    eye = jnp.eye(g, dtype=blocks.dtype)
    return jnp.einsum('gh,gcd->gchd', eye, blocks).reshape(g * n, g * n)


def _prep_layer(w_in, w_uq, w_ukv, pool_w):
    D = w_in.shape[0]
    offs = [0]
    for sz in (POOL_WIDTH, Q_LORA, KV_LORA, QK_ROPE, CONV_WIDTH, CONV_WIDTH, FOURIER_WIDTH, GATES_WIDTH,
               N_BRANCH * D_MODEL):
        offs.append(offs[-1] + sz)
    cols = [w_in[:, offs[i]:offs[i + 1]] for i in range(len(offs) - 1)]
    w_upool, w_cq, w_ckv, w_kr, w_ca, w_cb, w_uf, w_g, w_m = cols
    zl = jnp.zeros((D, QK_NOPE), F32)
    zr = jnp.zeros((D, HEAD_PAD - QK_DIM), F32)
    w_kra = jnp.concatenate([zl, w_kr, zr], axis=1)
    w_krb = jnp.concatenate([zl, _rot_half_cols(w_kr), zr], axis=1)
    w_a = jnp.concatenate([w_upool, w_cq, w_ckv, w_kra, w_krb, w_ca, w_cb, w_uf, w_g], axis=1).astype(BF16)

    wq = w_uq.reshape(Q_LORA, N_HEADS, QK_DIM)
    nope, ropew = wq[..., :QK_NOPE], wq[..., QK_NOPE:]
    zpad = jnp.zeros((Q_LORA, N_HEADS, HEAD_PAD - QK_DIM), F32)
    wq_a = jnp.concatenate([nope, ropew, zpad], axis=-1).reshape(Q_LORA, N_HEADS * HEAD_PAD)
    wq_b = jnp.concatenate([jnp.zeros_like(nope), _rot_half_cols(ropew), zpad], axis=-1)
    wq_b = wq_b.reshape(Q_LORA, N_HEADS * HEAD_PAD)
    wq_ab = jnp.concatenate([wq_a, wq_b], axis=1).astype(BF16)

    wkv = w_ukv.reshape(KV_LORA, N_HEADS, QK_NOPE + V_HEAD)
    wk = jnp.concatenate([wkv[..., :QK_NOPE], jnp.zeros((KV_LORA, N_HEADS, HEAD_PAD - QK_NOPE), F32)], axis=-1)
    wk = wk.reshape(KV_LORA, N_HEADS * HEAD_PAD).astype(BF16)
    wv = wkv[..., QK_NOPE:].reshape(KV_LORA, ATTN_WIDTH).astype(BF16)
    return w_a, w_m.astype(BF16), wq_ab, wk, wv, _block_diag(pool_w).astype(BF16)


def _rope_tables(S):
    inv_freq = 1.0 / (ROPE_BASE ** (jnp.arange(0, QK_ROPE, 2, dtype=F32) / QK_ROPE))
    ang = jnp.arange(S, dtype=F32)[:, None] * inv_freq[None, :]
    ang = jnp.concatenate([ang, ang], axis=-1)
    pad = jnp.zeros((S, HEAD_PAD - QK_DIM), F32)
    cos_t = jnp.concatenate([jnp.ones((S, QK_NOPE), F32), jnp.cos(ang), pad], axis=-1)
    sin_t = jnp.concatenate([jnp.zeros((S, QK_NOPE), F32), jnp.sin(ang), pad], axis=-1)
    return cos_t, sin_t


def _dft_tables(ks, n):
    ang = ((ks[:, None] * jnp.arange(n, dtype=jnp.int32)[None, :]) % n).astype(F32) * (2.0 * math.pi / n)
    return jnp.cos(ang), jnp.sin(ang)


def _group_size(B):
    for nb in (4, 2, 1):
        if B % nb == 0:
            return nb


def _tile(S, t):
    return t if S % t == 0 else S


def _layer(x, tabs, pre_g, post_g, w_in, gate_b, q_norm_g, w_uq, kv_norm_g, w_ukv, pool_w, pool_scale,
           conv_w, conv_b, conv_ln_g, conv_ln_b, conv_pw_w, conv_pw_b, fourier_w,
           w_up_pool, w_up_attn, w_up_conv, w_up_fourier, w_out):
    B, S, D = x.shape
    nb = _group_size(B)
    ts, tq = _tile(S, 512), _tile(S, 256)
    cos_t, sin_t, cbase, sbase, crow, srow, cb, sb = tabs
    tk = cbase.shape[0]
    row = lambda a: a.reshape(1, -1)
    w_a, w_m, wq_ab, wk, wv, pool_bd = _prep_layer(w_in, w_uq, w_ukv, pool_w)

    (u_pool, q, k, v, a_conv, u_four, g_pool, g_attn, g_conv, g_four) = _in_proj(
        x, row(pre_g), w_a, row(q_norm_g), wq_ab, row(kv_norm_g), wk, wv, cos_t, sin_t, ts=ts, nb=nb)
    y_pool = _pool(u_pool, g_pool, pool_bd, row(pool_scale))
    y_conv = _conv(a_conv, g_conv, conv_w, row(conv_b), row(conv_ln_g), row(conv_ln_b),
                   conv_pw_w.astype(BF16), row(conv_pw_b))
    y_four = _fourier(cbase, sbase, crow, srow, u_four, cb, sb, fourier_w.astype(BF16), g_four, tk=tk, nb=nb)
    y_attn = _attention(q, k, v, g_attn, tq=tq)
    return _merge(x, row(pre_g), w_m, row(gate_b), y_pool, y_attn, y_conv, y_four,
                  w_up_pool.astype(BF16), w_up_attn.astype(BF16), w_up_conv.astype(BF16),
                  w_up_fourier.astype(BF16), w_out.astype(BF16), row(post_g), ts=ts, nb=nb)


DFT_ROWS = 256


def _tables(S):
    cos_t, sin_t = _rope_tables(S)
    tk = _tile(S, DFT_ROWS)
    cbase, sbase = _dft_tables(jnp.arange(tk, dtype=jnp.int32), S)
    crow, srow = _dft_tables(jnp.arange(0, S, tk, dtype=jnp.int32), S)
    crow, srow = crow.reshape(S // tk, 1, S), srow.reshape(S // tk, 1, S)
    c_ch, s_ch = _dft_tables(jnp.arange(FOURIER_GROUP_DIM, dtype=jnp.int32), FOURIER_GROUP_DIM)
    norm = 1.0 / math.sqrt(S * FOURIER_GROUP_DIM)
    eye = jnp.eye(FOURIER_GROUPS, dtype=F32)
    cb = jnp.kron(eye, c_ch * norm).astype(BF16)
    sb = jnp.kron(eye, s_ch * norm).astype(BF16)
    return cos_t, sin_t, cbase, sbase, crow, srow, cb, sb


def _trunk(x, tabs, layer_weights):
    depth = layer_weights[0].shape[0]
    for l in range(depth):
        x = _layer(x, tabs, *[w[l] for w in layer_weights])
    return x


def kernel(x_prompt, x_sample, pre_norm_g, post_norm_g, w_in, gate_b, q_norm_g, w_uq, kv_norm_g, w_ukv, pool_w, pool_scale, conv_w, conv_b, conv_ln_g, conv_ln_b, conv_pw_w, conv_pw_b, fourier_w, w_up_pool, w_up_attn, w_up_conv, w_up_fourier, w_out):
    weights = (pre_norm_g, post_norm_g, w_in, gate_b, q_norm_g, w_uq, kv_norm_g, w_ukv, pool_w, pool_scale,
               conv_w, conv_b, conv_ln_g, conv_ln_b, conv_pw_w, conv_pw_b, fourier_w,
               w_up_pool, w_up_attn, w_up_conv, w_up_fourier, w_out)
    tabs = _tables(x_prompt.shape[1])
    assert x_sample.shape[1] == x_prompt.shape[1]
    return (_trunk(x_prompt, tabs, weights), _trunk(x_sample, tabs, weights))
```

```python
import functools
import math

import jax
import jax.numpy as jnp
from jax import lax
from jax.experimental import pallas as pl
from jax.experimental.pallas import tpu as pltpu

F32 = jnp.float32
BF16 = jnp.bfloat16

D_MODEL = 1024
POOL_WIDTH = 256
POOL_GROUPS = 4
POOL_GROUP_DIM = POOL_WIDTH // POOL_GROUPS
POOL_WINDOWS = (2, 4, 8, 16)
N_HEADS = 8
QK_NOPE = 64
QK_ROPE = 32
QK_DIM = QK_NOPE + QK_ROPE
V_HEAD = 64
Q_LORA = 256
KV_LORA = 128
ROPE_BASE = 10000.0
ATTN_WIDTH = N_HEADS * V_HEAD
CONV_WIDTH = 256
CONV_K = 31
FOURIER_WIDTH = 256
FOURIER_GROUPS = 4
FOURIER_GROUP_DIM = FOURIER_WIDTH // FOURIER_GROUPS
N_BRANCH = 4
NORM_EPS = 1e-6
LN_EPS = 1e-5

LANES = 128
SUBLANES = 8
HEAD_PAD = LANES
VMEM_LIMIT_BYTES = 56 * 1024 * 1024

Q_SCALE = (QK_DIM ** -0.5) * math.log2(math.e)

C_UPOOL = 0
C_CQ = C_UPOOL + POOL_WIDTH
C_CKV = C_CQ + Q_LORA
C_KR = C_CKV + KV_LORA
C_CONVA = C_KR + HEAD_PAD
C_CONVB = C_CONVA + CONV_WIDTH
C_UFOUR = C_CONVB + CONV_WIDTH
C_GATES = C_UFOUR + FOURIER_WIDTH
GATES_WIDTH = POOL_WIDTH + ATTN_WIDTH + CONV_WIDTH + FOURIER_WIDTH
C_END = C_GATES + GATES_WIDTH


def _params(*sem):
    return pltpu.CompilerParams(dimension_semantics=sem, vmem_limit_bytes=VMEM_LIMIT_BYTES)


def _dot(a, b):
    return jnp.dot(a, b, preferred_element_type=F32)


def _rms(x, g):
    return x * lax.rsqrt(jnp.mean(x * x, axis=-1, keepdims=True) + NORM_EPS) * g


def _sigmoid(x):
    return 1.0 / (1.0 + jnp.exp(-x))


def _rope(x, cos, sin_lo, sin_hi):
    return x * cos + pltpu.roll(x, HEAD_PAD - QK_ROPE // 2, 1) * sin_lo + pltpu.roll(x, QK_ROPE // 2, 1) * sin_hi


def _in_proj_kernel(x_ref, pre_g_ref, w_ref, qg_ref, wq_ref, kvg_ref, wk_ref, wv_ref, cos_ref, sinlo_ref, sinhi_ref,
                    upool_ref, q_ref, k_ref, v_ref, aconv_ref, ufour_ref,
                    gpool_ref, gattn_ref, gconv_ref, gfour_ref):
    h = _rms(x_ref[0], pre_g_ref[...]).astype(BF16)

    def proj(lo, hi):
        return _dot(h, w_ref[:, lo:hi])

    upool_ref[0] = proj(C_UPOOL, C_CQ)

    cos = cos_ref[...]
    sin_lo, sin_hi = sinlo_ref[...], sinhi_ref[...]

    cq = _rms(proj(C_CQ, C_CKV), qg_ref[...]).astype(BF16)
    q = _dot(cq, wq_ref[...])
    for hd in range(N_HEADS):
        sl = slice(hd * HEAD_PAD, (hd + 1) * HEAD_PAD)
        q_ref[0, :, sl] = (_rope(q[:, sl], cos, sin_lo, sin_hi) * Q_SCALE).astype(BF16)

    zk = proj(C_CKV, C_CONVA)
    ckv = _rms(zk[:, :KV_LORA], kvg_ref[...]).astype(BF16)
    kn = _dot(ckv, wk_ref[...])
    kr = _rope(zk[:, KV_LORA:], cos, sin_lo, sin_hi)
    for hd in range(N_HEADS):
        sl = slice(hd * HEAD_PAD, (hd + 1) * HEAD_PAD)
        k_ref[0, :, sl] = (kn[:, sl] + kr).astype(BF16)
    v_ref[0] = _dot(ckv, wv_ref[...]).astype(BF16)

    aconv_ref[0] = proj(C_CONVA, C_CONVB) * _sigmoid(proj(C_CONVB, C_UFOUR))
    ufour_ref[0] = proj(C_UFOUR, C_GATES).astype(BF16)

    def gate(lo, hi):
        z = proj(C_GATES + lo, C_GATES + hi)
        return (z * _sigmoid(z)).astype(BF16)

    o1 = POOL_WIDTH
    o2 = o1 + ATTN_WIDTH
    o3 = o2 + CONV_WIDTH
    gpool_ref[0] = gate(0, o1)
    gattn_ref[0] = gate(o1, o2)
    gconv_ref[0] = gate(o2, o3)
    gfour_ref[0] = gate(o3, GATES_WIDTH)


def _in_proj(x, pre_g, w_a, qg, wq, kvg, wk, wv, cos_t, sin_lo, sin_hi, *, ts, nb):
    B, S, D = x.shape
    grid = (B, S // ts)
    tok = lambda w: pl.BlockSpec((1, ts, w), lambda b, s: (b, s, 0))
    grp = lambda w: pl.BlockSpec((1, ts, w), lambda b, s: (b // nb, s, b % nb))
    full = lambda a: pl.BlockSpec(a.shape, lambda b, s: (0,) * a.ndim)
    rope = pl.BlockSpec((ts, HEAD_PAD), lambda b, s: (s, 0))
    sds = jax.ShapeDtypeStruct
    out_shape = (
        sds((B, S, POOL_WIDTH), F32),
        sds((B, S, N_HEADS * HEAD_PAD), BF16),
        sds((B, S, N_HEADS * HEAD_PAD), BF16),
        sds((B, S, ATTN_WIDTH), BF16),
        sds((B, S, CONV_WIDTH), F32),
        sds((B // nb, S, nb * FOURIER_WIDTH), BF16),
        sds((B, S, POOL_WIDTH), BF16),
        sds((B, S, ATTN_WIDTH), BF16),
        sds((B, S, CONV_WIDTH), BF16),
        sds((B // nb, S, nb * FOURIER_WIDTH), BF16),
    )
    out_specs = (tok(POOL_WIDTH), tok(N_HEADS * HEAD_PAD), tok(N_HEADS * HEAD_PAD), tok(ATTN_WIDTH),
                 tok(CONV_WIDTH), grp(FOURIER_WIDTH), tok(POOL_WIDTH), tok(ATTN_WIDTH), tok(CONV_WIDTH),
                 grp(FOURIER_WIDTH))
    return pl.pallas_call(
        _in_proj_kernel, grid=grid,
        in_specs=[tok(D), full(pre_g), full(w_a), full(qg), full(wq), full(kvg), full(wk), full(wv), rope, rope, rope],
        out_specs=out_specs, out_shape=out_shape,
        compiler_params=_params("parallel", "parallel"), name="in_proj",
    )(x, pre_g, w_a, qg, wq, kvg, wk, wv, cos_t, sin_lo, sin_hi)


POOL_HALO = max(POOL_WINDOWS) // 2
POOL_CHUNK = 256


def _pool_kernel(u_ref, g_ref, w_ref, scale_ref, y_ref, pad_ref, *, S):
    zeros = jnp.zeros((POOL_HALO, POOL_WIDTH), F32)
    pad_ref[0:POOL_HALO, :] = zeros
    pad_ref[POOL_HALO + S:POOL_HALO + S + POOL_HALO, :] = zeros
    pad_ref[POOL_HALO:POOL_HALO + S, :] = u_ref[0]

    ch = min(POOL_CHUNK, S)
    lane_group = lax.broadcasted_iota(jnp.int32, (ch, POOL_WIDTH), 1) // POOL_GROUP_DIM
    half = jnp.left_shift(1, lane_group)
    for c in range(S // ch):
        base = c * ch

        def rows(off):
            return pad_ref[POOL_HALO + base + off:POOL_HALO + base + off + ch, :]

        u = rows(0)
        acc = u + rows(-1)
        sums = [acc]
        for w_prev, w_next in zip(POOL_WINDOWS[:-1], POOL_WINDOWS[1:]):
            for off in list(range(-w_next // 2, -w_prev // 2)) + list(range(w_prev // 2, w_next // 2)):
                acc = acc + rows(off)
            sums.append(acc)
        win = jnp.where(lane_group == 0, sums[0],
                        jnp.where(lane_group == 1, sums[1], jnp.where(lane_group == 2, sums[2], sums[3])))
        t = lax.broadcasted_iota(jnp.int32, (ch, POOL_WIDTH), 0) + base
        cnt = (jnp.minimum(t + half, S) - jnp.maximum(t - half, 0)).astype(F32)
        p = (win / cnt - u).astype(BF16)
        y = _dot(p, w_ref[...]) * scale_ref[...] * g_ref[0, base:base + ch, :].astype(F32)
        y_ref[0, base:base + ch, :] = y.astype(BF16)


def _pool(u, g, w_bd, scale):
    B, S, W = u.shape
    seq = pl.BlockSpec((1, S, W), lambda b: (b, 0, 0))
    full = lambda a: pl.BlockSpec(a.shape, lambda b: (0,) * a.ndim)
    return pl.pallas_call(
        functools.partial(_pool_kernel, S=S), grid=(B,),
        in_specs=[seq, seq, full(w_bd), full(scale)], out_specs=seq,
        out_shape=jax.ShapeDtypeStruct((B, S, W), BF16),
        scratch_shapes=[pltpu.VMEM((S + 2 * POOL_HALO, W), F32)],
        compiler_params=_params("parallel"), name="pool",
    )(u, g, w_bd, scale)


CONV_HALO = 16
CONV_CHUNK = 512
CONV_ROWS = 64
CONV_FIRST = CONV_HALO - CONV_K // 2
CONV_SPAN = (CONV_FIRST + CONV_K - 1) // SUBLANES * SUBLANES


def _conv_kernel(a_ref, g_ref, cw_ref, cb_ref, lng_ref, lnb_ref, pw_ref, pwb_ref, y_ref, pad_ref, sh_ref, acc_ref,
                 *, S):
    zeros = jnp.zeros((CONV_HALO, CONV_WIDTH), F32)
    pad_ref[0:CONV_HALO, :] = zeros
    pad_ref[CONV_HALO + S:CONV_HALO + S + CONV_HALO, :] = zeros
    pad_ref[CONV_HALO:CONV_HALO + S, :] = a_ref[0]

    ch = min(CONV_CHUNK, S)
    rb = min(CONV_ROWS, ch)
    for c in range(S // ch):
        base = c * ch
        for r in range(1, SUBLANES):
            sh_ref[r - 1] = pad_ref[base + r:base + r + ch + CONV_SPAN, :]
        for sb in range(ch // rb):
            row = sb * rb
            acc = jnp.zeros((rb, CONV_WIDTH), F32) + cb_ref[...]
            for k in range(CONV_K):
                off = CONV_FIRST + k
                al, r = off // SUBLANES * SUBLANES, off % SUBLANES
                if r == 0:
                    x = pad_ref[base + row + al:base + row + al + rb, :]
                else:
                    x = sh_ref[r - 1, row + al:row + al + rb, :]
                acc = acc + x * cw_ref[k:k + 1, :]
            acc_ref[row:row + rb, :] = acc
        y = acc_ref[...]
        mu = jnp.mean(y, axis=-1, keepdims=True)
        d = y - mu
        var = jnp.mean(d * d, axis=-1, keepdims=True)
        yn = d * lax.rsqrt(var + LN_EPS) * lng_ref[...] + lnb_ref[...]
        act = (yn * _sigmoid(yn)).astype(BF16)
        out = (_dot(act, pw_ref[...]) + pwb_ref[...]) * g_ref[0, base:base + ch, :].astype(F32)
        y_ref[0, base:base + ch, :] = out.astype(BF16)


def _conv(a, g, cw, cb, lng, lnb, pw, pwb):
    B, S, W = a.shape
    ch = min(CONV_CHUNK, S)
    seq = pl.BlockSpec((1, S, W), lambda b: (b, 0, 0))
    full = lambda t: pl.BlockSpec(t.shape, lambda b: (0,) * t.ndim)
    return pl.pallas_call(
        functools.partial(_conv_kernel, S=S), grid=(B,),
        in_specs=[seq, seq, full(cw), full(cb), full(lng), full(lnb), full(pw), full(pwb)], out_specs=seq,
        out_shape=jax.ShapeDtypeStruct((B, S, W), BF16),
        scratch_shapes=[pltpu.VMEM((S + 2 * CONV_HALO, W), F32),
                        pltpu.VMEM((SUBLANES - 1, ch + CONV_SPAN, W), F32),
                        pltpu.VMEM((ch, W), F32)],
        compiler_params=_params("parallel"), name="conv",
    )(a, g, cw, cb, lng, lnb, pw, pwb)


def _fourier_kernel(cbase_ref, sbase_ref, crow_ref, srow_ref, u_ref, cb_ref, sb_ref, fw_ref, rev_ref,
                    glo_ref, ghi_ref, ylo_ref, yhi_ref, *, nb):
    tk = ylo_ref.shape[1]
    c0, s0 = crow_ref[0], srow_ref[0]
    cbase, sbase = cbase_ref[...], sbase_ref[...]
    cmat = (cbase * c0 - sbase * s0).astype(BF16)
    smat = (sbase * c0 + cbase * s0).astype(BF16)
    u = u_ref[0]
    a = _dot(cmat, u).astype(BF16)
    b = _dot(smat, u).astype(BF16)
    for n in range(nb):
        sl = slice(n * FOURIER_WIDTH, (n + 1) * FOURIER_WIDTH)
        cc = _dot(a[:, sl], cb_ref[...])
        ss = _dot(b[:, sl], sb_ref[...])
        f = (cc - ss)[:tk].astype(BF16)
        fm = _dot(rev_ref[...], (cc + ss).astype(BF16)).astype(BF16)
        ylo_ref[0, :, sl] = (_dot(f, fw_ref[...]) * glo_ref[0, :, sl].astype(F32)).astype(BF16)
        yhi_ref[0, :, sl] = (_dot(fm, fw_ref[...]) * ghi_ref[0, :, sl].astype(F32)).astype(BF16)


def _fourier(cbase, sbase, crow, srow, u, cb, sb, fw, rev, g, *, nb):
    G, S, W = u.shape
    tk = rev.shape[0]
    nt = S // tk
    assert nt % 2 == 0
    full = lambda t: pl.BlockSpec(t.shape, lambda i, j: (0,) * t.ndim)
    row = pl.BlockSpec((1, 1, S), lambda i, j: (j, 0, 0))
    lo = pl.BlockSpec((1, tk, W), lambda i, j: (i, j, 0))
    hi_in = pl.BlockSpec((1, tk, W), lambda i, j: (i, nt - 1 - j, 0))
    hi_out = pl.BlockSpec((1, tk, W), lambda i, j: (i, nt // 2 - 1 - j, 0))
    half = jax.ShapeDtypeStruct((G, S // 2, W), BF16)
    ylo, yhi = pl.pallas_call(
        functools.partial(_fourier_kernel, nb=nb), grid=(G, nt // 2),
        in_specs=[full(cbase), full(sbase), row, row, pl.BlockSpec((1, S, W), lambda i, j: (i, 0, 0)),
                  full(cb), full(sb), full(fw), full(rev), lo, hi_in],
        out_specs=(lo, hi_out), out_shape=(half, half),
        compiler_params=_params("parallel", "parallel"), name="fourier",
    )(cbase, sbase, crow, srow, u, cb, sb, fw, rev, g, g)
    return jnp.concatenate([ylo, yhi], axis=1)


def _attn_kernel(q_ref, k_ref, v_ref, g_ref, o_ref):
    def scores(h):
        sl = slice(h * HEAD_PAD, (h + 1) * HEAD_PAD)
        return lax.dot_general(q_ref[0, :, sl], k_ref[0, :, sl], (((1,), (1,)), ((), ())),
                               preferred_element_type=F32)

    def finish(h, s):
        v = v_ref[0, :, h // 2 * LANES:(h // 2 + 1) * LANES]
        m = jnp.max(s, axis=-1, keepdims=True)
        p = jnp.exp2(s - m)
        l = jnp.sum(p, axis=-1, keepdims=True)
        return _dot(p.astype(BF16), v) / l

    outs = []
    s_prev = scores(0)
    for h in range(1, N_HEADS):
        s_next = scores(h)
        outs.append(finish(h - 1, s_prev))
        s_prev = s_next
    outs.append(finish(N_HEADS - 1, s_prev))
    lane = lax.broadcasted_iota(jnp.int32, outs[0].shape, 1)
    for pr in range(N_HEADS // 2):
        o = jnp.where(lane < V_HEAD, outs[2 * pr], outs[2 * pr + 1])
        sl = slice(pr * LANES, (pr + 1) * LANES)
        o_ref[0, :, sl] = (o * g_ref[0, :, sl].astype(F32)).astype(BF16)


def _attention(q, k, v, g, *, tq):
    B, S, _ = q.shape
    qblk = pl.BlockSpec((1, tq, N_HEADS * HEAD_PAD), lambda b, i: (b, i, 0))
    oblk = pl.BlockSpec((1, tq, ATTN_WIDTH), lambda b, i: (b, i, 0))
    return pl.pallas_call(
        _attn_kernel, grid=(B, S // tq),
        in_specs=[qblk,
                  pl.BlockSpec((1, S, N_HEADS * HEAD_PAD), lambda b, i: (b, 0, 0)),
                  pl.BlockSpec((1, S, ATTN_WIDTH), lambda b, i: (b, 0, 0)),
                  oblk],
        out_specs=oblk,
        out_shape=jax.ShapeDtypeStruct((B, S, ATTN_WIDTH), BF16),
        compiler_params=_params("parallel", "parallel"), name="attention",
    )(q, k, v, g)


def _merge_kernel(x_ref, pre_g_ref, wm_ref, gb_ref, yp_ref, ya_ref, yc_ref, yf_ref,
                  wup_ref, wua_ref, wuc_ref, wuf_ref, wo_ref, post_g_ref, o_ref):
    x = x_ref[0]
    h = _rms(x, pre_g_ref[...]).astype(BF16)
    m = None
    branches = ((yp_ref, wup_ref), (ya_ref, wua_ref), (yc_ref, wuc_ref), (yf_ref, wuf_ref))
    for i, (y_ref, wu_ref) in enumerate(branches):
        sl = slice(i * D_MODEL, (i + 1) * D_MODEL)
        gate = _sigmoid(_dot(h, wm_ref[:, sl]) + gb_ref[:, sl])
        term = gate * _dot(y_ref[0], wu_ref[...])
        m = term if m is None else m + term
    out = _dot(m.astype(BF16), wo_ref[...])
    o_ref[0] = x + _rms(out, post_g_ref[...])


def _merge(x, pre_g, wm, gb, yp, ya, yc, yf, wup, wua, wuc, wuf, wo, post_g, *, ts, nb):
    B, S, D = x.shape
    tok = lambda w: pl.BlockSpec((1, ts, w), lambda b, s: (b, s, 0))
    grp = lambda w: pl.BlockSpec((1, ts, w), lambda b, s: (b // nb, s, b % nb))
    full = lambda a: pl.BlockSpec(a.shape, lambda b, s: (0,) * a.ndim)
    return pl.pallas_call(
        _merge_kernel, grid=(B, S // ts),
        in_specs=[tok(D), full(pre_g), full(wm), full(gb), tok(POOL_WIDTH), tok(ATTN_WIDTH), tok(CONV_WIDTH),
                  grp(FOURIER_WIDTH), full(wup), full(wua), full(wuc), full(wuf), full(wo), full(post_g)],
        out_specs=tok(D), out_shape=jax.ShapeDtypeStruct((B, S, D), F32),
        compiler_params=_params("parallel", "parallel"), name="merge",
    )(x, pre_g, wm, gb, yp, ya, yc, yf, wup, wua, wuc, wuf, wo, post_g)


def _block_diag(blocks):
    g, n, _ = blocks.shape
    eye = jnp.eye(g, dtype=blocks.dtype)
    return jnp.einsum('gh,gcd->gchd', eye, blocks).reshape(g * n, g * n)


def _prep_layer(w_in, w_uq, w_ukv, pool_w):
    D = w_in.shape[0]
    offs = [0]
    for sz in (POOL_WIDTH, Q_LORA, KV_LORA, QK_ROPE, CONV_WIDTH, CONV_WIDTH, FOURIER_WIDTH, GATES_WIDTH,
               N_BRANCH * D_MODEL):
        offs.append(offs[-1] + sz)
    cols = [w_in[:, offs[i]:offs[i + 1]] for i in range(len(offs) - 1)]
    w_upool, w_cq, w_ckv, w_kr, w_ca, w_cb, w_uf, w_g, w_m = cols
    zl = jnp.zeros((D, QK_NOPE), F32)
    zr = jnp.zeros((D, HEAD_PAD - QK_DIM), F32)
    w_krp = jnp.concatenate([zl, w_kr, zr], axis=1)
    w_a = jnp.concatenate([w_upool, w_cq, w_ckv, w_krp, w_ca, w_cb, w_uf, w_g], axis=1).astype(BF16)

    wq = w_uq.reshape(Q_LORA, N_HEADS, QK_DIM)
    zpad = jnp.zeros((Q_LORA, N_HEADS, HEAD_PAD - QK_DIM), F32)
    wq_p = jnp.concatenate([wq, zpad], axis=-1).reshape(Q_LORA, N_HEADS * HEAD_PAD).astype(BF16)

    wkv = w_ukv.reshape(KV_LORA, N_HEADS, QK_NOPE + V_HEAD)
    wk = jnp.concatenate([wkv[..., :QK_NOPE], jnp.zeros((KV_LORA, N_HEADS, HEAD_PAD - QK_NOPE), F32)], axis=-1)
    wk = wk.reshape(KV_LORA, N_HEADS * HEAD_PAD).astype(BF16)
    wv = wkv[..., QK_NOPE:].reshape(KV_LORA, ATTN_WIDTH).astype(BF16)
    return w_a, w_m.astype(BF16), wq_p, wk, wv, _block_diag(pool_w).astype(BF16)


def _rope_tables(S):
    inv_freq = 1.0 / (ROPE_BASE ** (jnp.arange(0, QK_ROPE, 2, dtype=F32) / QK_ROPE))
    ang = jnp.arange(S, dtype=F32)[:, None] * inv_freq[None, :]
    ang = jnp.concatenate([ang, ang], axis=-1)
    half = QK_ROPE // 2
    sin = jnp.sin(ang)
    zero = lambda w: jnp.zeros((S, w), F32)
    cos_t = jnp.concatenate([jnp.ones((S, QK_NOPE), F32), jnp.cos(ang), zero(HEAD_PAD - QK_DIM)], axis=-1)
    sin_lo = jnp.concatenate([zero(QK_NOPE), -sin[:, :half], zero(HEAD_PAD - QK_NOPE - half)], axis=-1)
    sin_hi = jnp.concatenate([zero(QK_NOPE + half), sin[:, half:], zero(HEAD_PAD - QK_DIM)], axis=-1)
    return cos_t, sin_lo, sin_hi


def _dft_tables(ks, n):
    ang = ((ks[:, None] * jnp.arange(n, dtype=jnp.int32)[None, :]) % n).astype(F32) * (2.0 * math.pi / n)
    return jnp.cos(ang), jnp.sin(ang)


def _group_size(B):
    for nb in (4, 2, 1):
        if B % nb == 0:
            return nb


def _tile(S, t):
    return t if S % t == 0 else S


def _layer(x, tabs, pre_g, post_g, w_in, gate_b, q_norm_g, w_uq, kv_norm_g, w_ukv, pool_w, pool_scale,
           conv_w, conv_b, conv_ln_g, conv_ln_b, conv_pw_w, conv_pw_b, fourier_w,
           w_up_pool, w_up_attn, w_up_conv, w_up_fourier, w_out):
    B, S, D = x.shape
    nb = _group_size(B)
    ts, tq = _tile(S, 512), _tile(S, 256)
    cos_t, sin_lo, sin_hi, cbase, sbase, crow, srow, cb, sb, rev = tabs
    row = lambda a: a.reshape(1, -1)
    w_a, w_m, wq_p, wk, wv, pool_bd = _prep_layer(w_in, w_uq, w_ukv, pool_w)

    (u_pool, q, k, v, a_conv, u_four, g_pool, g_attn, g_conv, g_four) = _in_proj(
        x, row(pre_g), w_a, row(q_norm_g), wq_p, row(kv_norm_g), wk, wv, cos_t, sin_lo, sin_hi, ts=ts, nb=nb)
    y_pool = _pool(u_pool, g_pool, pool_bd, row(pool_scale))
    y_conv = _conv(a_conv, g_conv, conv_w, row(conv_b), row(conv_ln_g), row(conv_ln_b),
                   conv_pw_w.astype(BF16), row(conv_pw_b))
    y_four = _fourier(cbase, sbase, crow, srow, u_four, cb, sb, fourier_w.astype(BF16), rev, g_four, nb=nb)
    y_attn = _attention(q, k, v, g_attn, tq=tq)
    return _merge(x, row(pre_g), w_m, row(gate_b), y_pool, y_attn, y_conv, y_four,
                  w_up_pool.astype(BF16), w_up_attn.astype(BF16), w_up_conv.astype(BF16),
                  w_up_fourier.astype(BF16), w_out.astype(BF16), row(post_g), ts=ts, nb=nb)


DFT_ROWS = 256


def _tables(S):
    cos_t, sin_lo, sin_hi = _rope_tables(S)
    tk = _tile(S, DFT_ROWS)
    ext = tk + SUBLANES
    cbase, sbase = _dft_tables(jnp.arange(ext, dtype=jnp.int32), S)
    rev = (jnp.arange(tk)[:, None] + jnp.arange(ext)[None, :] == tk).astype(BF16)
    crow, srow = _dft_tables(jnp.arange(0, S, tk, dtype=jnp.int32), S)
    crow, srow = crow.reshape(S // tk, 1, S), srow.reshape(S // tk, 1, S)
    c_ch, s_ch = _dft_tables(jnp.arange(FOURIER_GROUP_DIM, dtype=jnp.int32), FOURIER_GROUP_DIM)
    norm = 1.0 / math.sqrt(S * FOURIER_GROUP_DIM)
    eye = jnp.eye(FOURIER_GROUPS, dtype=F32)
    cb = jnp.kron(eye, c_ch * norm).astype(BF16)
    sb = jnp.kron(eye, s_ch * norm).astype(BF16)
    return cos_t, sin_lo, sin_hi, cbase, sbase, crow, srow, cb, sb, rev


def _trunk(x, tabs, layer_weights):
    depth = layer_weights[0].shape[0]
    for l in range(depth):
        x = _layer(x, tabs, *[w[l] for w in layer_weights])
    return x


def kernel(x_prompt, x_sample, pre_norm_g, post_norm_g, w_in, gate_b, q_norm_g, w_uq, kv_norm_g, w_ukv, pool_w, pool_scale, conv_w, conv_b, conv_ln_g, conv_ln_b, conv_pw_w, conv_pw_b, fourier_w, w_up_pool, w_up_attn, w_up_conv, w_up_fourier, w_out):
    weights = (pre_norm_g, post_norm_g, w_in, gate_b, q_norm_g, w_uq, kv_norm_g, w_ukv, pool_w, pool_scale,
               conv_w, conv_b, conv_ln_g, conv_ln_b, conv_pw_w, conv_pw_b, fourier_w,
               w_up_pool, w_up_attn, w_up_conv, w_up_fourier, w_out)
    tabs = _tables(x_prompt.shape[1])
    assert x_sample.shape[1] == x_prompt.shape[1]
    return (_trunk(x_prompt, tabs, weights), _trunk(x_sample, tabs, weights))
```

```python
import functools
import math

import jax
import jax.numpy as jnp
from jax import lax
from jax.experimental import pallas as pl
from jax.experimental.pallas import tpu as pltpu

F32 = jnp.float32
BF16 = jnp.bfloat16

D_MODEL = 1024
POOL_WIDTH = 256
POOL_GROUPS = 4
POOL_GROUP_DIM = POOL_WIDTH // POOL_GROUPS
POOL_WINDOWS = (2, 4, 8, 16)
N_HEADS = 8
QK_NOPE = 64
QK_ROPE = 32
QK_DIM = QK_NOPE + QK_ROPE
V_HEAD = 64
Q_LORA = 256
KV_LORA = 128
ROPE_BASE = 10000.0
ATTN_WIDTH = N_HEADS * V_HEAD
CONV_WIDTH = 256
CONV_K = 31
FOURIER_WIDTH = 256
FOURIER_GROUPS = 4
FOURIER_GROUP_DIM = FOURIER_WIDTH // FOURIER_GROUPS
N_BRANCH = 4
NORM_EPS = 1e-6
LN_EPS = 1e-5

LANES = 128
SUBLANES = 8
HEAD_PAD = LANES
VMEM_LIMIT_BYTES = 56 * 1024 * 1024

Q_SCALE = (QK_DIM ** -0.5) * math.log2(math.e)

C_UPOOL = 0
C_CQ = C_UPOOL + POOL_WIDTH
C_CKV = C_CQ + Q_LORA
C_KR = C_CKV + KV_LORA
C_CONVA = C_KR + HEAD_PAD
C_CONVB = C_CONVA + CONV_WIDTH
C_UFOUR = C_CONVB + CONV_WIDTH
C_GATES = C_UFOUR + FOURIER_WIDTH
GATES_WIDTH = POOL_WIDTH + ATTN_WIDTH + CONV_WIDTH + FOURIER_WIDTH
C_END = C_GATES + GATES_WIDTH


def _params(*sem):
    return pltpu.CompilerParams(dimension_semantics=sem, vmem_limit_bytes=VMEM_LIMIT_BYTES)


def _dot(a, b):
    return jnp.dot(a, b, preferred_element_type=F32)


def _rms(x, g):
    return x * lax.rsqrt(jnp.mean(x * x, axis=-1, keepdims=True) + NORM_EPS) * g


def _sigmoid(x):
    return 1.0 / (1.0 + jnp.exp(-x))


def _rope(x, cos, sin_lo, sin_hi):
    return x * cos + pltpu.roll(x, HEAD_PAD - QK_ROPE // 2, 1) * sin_lo + pltpu.roll(x, QK_ROPE // 2, 1) * sin_hi


def _in_proj_kernel(x_ref, pre_g_ref, w_ref, qg_ref, wq_ref, kvg_ref, wk_ref, wvt_ref, cos_ref, sinlo_ref, sinhi_ref,
                    upool_ref, q_ref, k_ref, vt_ref, aconv_ref, ufour_ref,
                    gpool_ref, gattn_ref, gconv_ref, gfour_ref):
    h = _rms(x_ref[0], pre_g_ref[...]).astype(BF16)

    def proj(lo, hi):
        return _dot(h, w_ref[:, lo:hi])

    upool_ref[0] = proj(C_UPOOL, C_CQ)

    cos = cos_ref[...]
    sin_lo, sin_hi = sinlo_ref[...], sinhi_ref[...]

    cq = _rms(proj(C_CQ, C_CKV), qg_ref[...]).astype(BF16)
    q = _dot(cq, wq_ref[...])
    for hd in range(N_HEADS):
        sl = slice(hd * HEAD_PAD, (hd + 1) * HEAD_PAD)
        q_ref[0, :, sl] = (_rope(q[:, sl], cos, sin_lo, sin_hi) * Q_SCALE).astype(BF16)

    zk = proj(C_CKV, C_CONVA)
    ckv = _rms(zk[:, :KV_LORA], kvg_ref[...]).astype(BF16)
    kn = _dot(ckv, wk_ref[...])
    kr = _rope(zk[:, KV_LORA:], cos, sin_lo, sin_hi)
    for hd in range(N_HEADS):
        sl = slice(hd * HEAD_PAD, (hd + 1) * HEAD_PAD)
        k_ref[0, :, sl] = (kn[:, sl] + kr).astype(BF16)
    vt_ref[0] = lax.dot_general(wvt_ref[...], ckv, (((1,), (1,)), ((), ())),
                                preferred_element_type=F32).astype(BF16)

    aconv_ref[0] = proj(C_CONVA, C_CONVB) * _sigmoid(proj(C_CONVB, C_UFOUR))
    ufour_ref[0] = proj(C_UFOUR, C_GATES).astype(BF16)

    def gate(lo, hi):
        z = proj(C_GATES + lo, C_GATES + hi)
        return (z * _sigmoid(z)).astype(BF16)

    o1 = POOL_WIDTH
    o2 = o1 + ATTN_WIDTH
    o3 = o2 + CONV_WIDTH
    gpool_ref[0] = gate(0, o1)
    gattn_ref[0] = gate(o1, o2)
    gconv_ref[0] = gate(o2, o3)
    gfour_ref[0] = gate(o3, GATES_WIDTH)


def _in_proj(x, pre_g, w_a, qg, wq, kvg, wk, wvt, cos_t, sin_lo, sin_hi, *, ts, nb):
    B, S, D = x.shape
    grid = (B, S // ts)
    tok = lambda w: pl.BlockSpec((1, ts, w), lambda b, s: (b, s, 0))
    grp = lambda w: pl.BlockSpec((1, ts, w), lambda b, s: (b // nb, s, b % nb))
    full = lambda a: pl.BlockSpec(a.shape, lambda b, s: (0,) * a.ndim)
    rope = pl.BlockSpec((ts, HEAD_PAD), lambda b, s: (s, 0))
    sds = jax.ShapeDtypeStruct
    out_shape = (
        sds((B, S, POOL_WIDTH), F32),
        sds((B, S, N_HEADS * HEAD_PAD), BF16),
        sds((B, S, N_HEADS * HEAD_PAD), BF16),
        sds((B, ATTN_WIDTH, S), BF16),
        sds((B, S, CONV_WIDTH), F32),
        sds((B // nb, S, nb * FOURIER_WIDTH), BF16),
        sds((B, S, POOL_WIDTH), BF16),
        sds((B, S, ATTN_WIDTH), BF16),
        sds((B, S, CONV_WIDTH), BF16),
        sds((B // nb, S, nb * FOURIER_WIDTH), BF16),
    )
    vt_spec = pl.BlockSpec((1, ATTN_WIDTH, ts), lambda b, s: (b, 0, s))
    out_specs = (tok(POOL_WIDTH), tok(N_HEADS * HEAD_PAD), tok(N_HEADS * HEAD_PAD), vt_spec,
                 tok(CONV_WIDTH), grp(FOURIER_WIDTH), tok(POOL_WIDTH), tok(ATTN_WIDTH), tok(CONV_WIDTH),
                 grp(FOURIER_WIDTH))
    return pl.pallas_call(
        _in_proj_kernel, grid=grid,
        in_specs=[tok(D), full(pre_g), full(w_a), full(qg), full(wq), full(kvg), full(wk), full(wvt), rope, rope, rope],
        out_specs=out_specs, out_shape=out_shape,
        compiler_params=_params("parallel", "parallel"), name="in_proj",
    )(x, pre_g, w_a, qg, wq, kvg, wk, wvt, cos_t, sin_lo, sin_hi)


POOL_HALO = max(POOL_WINDOWS) // 2
POOL_CHUNK = 256


def _pool_kernel(u_ref, g_ref, w_ref, scale_ref, y_ref, pad_ref, *, S):
    zeros = jnp.zeros((POOL_HALO, POOL_WIDTH), F32)
    pad_ref[0:POOL_HALO, :] = zeros
    pad_ref[POOL_HALO + S:POOL_HALO + S + POOL_HALO, :] = zeros
    pad_ref[POOL_HALO:POOL_HALO + S, :] = u_ref[0]

    ch = min(POOL_CHUNK, S)
    lane_group = lax.broadcasted_iota(jnp.int32, (ch, POOL_WIDTH), 1) // POOL_GROUP_DIM
    half = jnp.left_shift(1, lane_group)
    for c in range(S // ch):
        base = c * ch

        def rows(off):
            return pad_ref[POOL_HALO + base + off:POOL_HALO + base + off + ch, :]

        u = rows(0)
        acc = u + rows(-1)
        sums = [acc]
        for w_prev, w_next in zip(POOL_WINDOWS[:-1], POOL_WINDOWS[1:]):
            for off in list(range(-w_next // 2, -w_prev // 2)) + list(range(w_prev // 2, w_next // 2)):
                acc = acc + rows(off)
            sums.append(acc)
        win = jnp.where(lane_group == 0, sums[0],
                        jnp.where(lane_group == 1, sums[1], jnp.where(lane_group == 2, sums[2], sums[3])))
        t = lax.broadcasted_iota(jnp.int32, (ch, POOL_WIDTH), 0) + base
        cnt = (jnp.minimum(t + half, S) - jnp.maximum(t - half, 0)).astype(F32)
        p = (win / cnt - u).astype(BF16)
        y = _dot(p, w_ref[...]) * scale_ref[...] * g_ref[0, base:base + ch, :].astype(F32)
        y_ref[0, base:base + ch, :] = y.astype(BF16)


def _pool(u, g, w_bd, scale):
    B, S, W = u.shape
    seq = pl.BlockSpec((1, S, W), lambda b: (b, 0, 0))
    full = lambda a: pl.BlockSpec(a.shape, lambda b: (0,) * a.ndim)
    return pl.pallas_call(
        functools.partial(_pool_kernel, S=S), grid=(B,),
        in_specs=[seq, seq, full(w_bd), full(scale)], out_specs=seq,
        out_shape=jax.ShapeDtypeStruct((B, S, W), BF16),
        scratch_shapes=[pltpu.VMEM((S + 2 * POOL_HALO, W), F32)],
        compiler_params=_params("parallel"), name="pool",
    )(u, g, w_bd, scale)


CONV_HALO = 16
CONV_CHUNK = 512
CONV_ROWS = 64
CONV_FIRST = CONV_HALO - CONV_K // 2
CONV_SPAN = (CONV_FIRST + CONV_K - 1) // SUBLANES * SUBLANES


def _conv_kernel(a_ref, g_ref, cw_ref, cb_ref, lng_ref, lnb_ref, pw_ref, pwb_ref, y_ref, pad_ref, sh_ref, acc_ref,
                 *, S):
    zeros = jnp.zeros((CONV_HALO, CONV_WIDTH), F32)
    pad_ref[0:CONV_HALO, :] = zeros
    pad_ref[CONV_HALO + S:CONV_HALO + S + CONV_HALO, :] = zeros
    pad_ref[CONV_HALO:CONV_HALO + S, :] = a_ref[0]

    ch = min(CONV_CHUNK, S)
    rb = min(CONV_ROWS, ch)
    for c in range(S // ch):
        base = c * ch
        for r in range(1, SUBLANES):
            sh_ref[r - 1] = pad_ref[base + r:base + r + ch + CONV_SPAN, :]
        for sb in range(ch // rb):
            row = sb * rb
            acc = jnp.zeros((rb, CONV_WIDTH), F32) + cb_ref[...]
            for k in range(CONV_K):
                off = CONV_FIRST + k
                al, r = off // SUBLANES * SUBLANES, off % SUBLANES
                if r == 0:
                    x = pad_ref[base + row + al:base + row + al + rb, :]
                else:
                    x = sh_ref[r - 1, row + al:row + al + rb, :]
                acc = acc + x * cw_ref[k:k + 1, :]
            acc_ref[row:row + rb, :] = acc
        y = acc_ref[...]
        mu = jnp.mean(y, axis=-1, keepdims=True)
        d = y - mu
        var = jnp.mean(d * d, axis=-1, keepdims=True)
        yn = d * lax.rsqrt(var + LN_EPS) * lng_ref[...] + lnb_ref[...]
        act = (yn * _sigmoid(yn)).astype(BF16)
        out = (_dot(act, pw_ref[...]) + pwb_ref[...]) * g_ref[0, base:base + ch, :].astype(F32)
        y_ref[0, base:base + ch, :] = out.astype(BF16)


def _conv(a, g, cw, cb, lng, lnb, pw, pwb):
    B, S, W = a.shape
    ch = min(CONV_CHUNK, S)
    seq = pl.BlockSpec((1, S, W), lambda b: (b, 0, 0))
    full = lambda t: pl.BlockSpec(t.shape, lambda b: (0,) * t.ndim)
    return pl.pallas_call(
        functools.partial(_conv_kernel, S=S), grid=(B,),
        in_specs=[seq, seq, full(cw), full(cb), full(lng), full(lnb), full(pw), full(pwb)], out_specs=seq,
        out_shape=jax.ShapeDtypeStruct((B, S, W), BF16),
        scratch_shapes=[pltpu.VMEM((S + 2 * CONV_HALO, W), F32),
                        pltpu.VMEM((SUBLANES - 1, ch + CONV_SPAN, W), F32),
                        pltpu.VMEM((ch, W), F32)],
        compiler_params=_params("parallel"), name="conv",
    )(a, g, cw, cb, lng, lnb, pw, pwb)


def _fourier_kernel(cbase_ref, sbase_ref, crow_ref, srow_ref, u_ref, cb_ref, sb_ref, fw_ref, rev_ref,
                    glo_ref, ghi_ref, ylo_ref, yhi_ref, *, nb):
    tk = ylo_ref.shape[1]
    c0, s0 = crow_ref[0], srow_ref[0]
    cbase, sbase = cbase_ref[...], sbase_ref[...]
    cmat = (cbase * c0 - sbase * s0).astype(BF16)
    smat = (sbase * c0 + cbase * s0).astype(BF16)
    u = u_ref[0]
    a = _dot(cmat, u).astype(BF16)
    b = _dot(smat, u).astype(BF16)
    for n in range(nb):
        sl = slice(n * FOURIER_WIDTH, (n + 1) * FOURIER_WIDTH)
        cc = _dot(a[:, sl], cb_ref[...])
        ss = _dot(b[:, sl], sb_ref[...])
        f = (cc - ss)[:tk].astype(BF16)
        fm = _dot(rev_ref[...], (cc + ss).astype(BF16)).astype(BF16)
        ylo_ref[0, :, sl] = (_dot(f, fw_ref[...]) * glo_ref[0, :, sl].astype(F32)).astype(BF16)
        yhi_ref[0, :, sl] = (_dot(fm, fw_ref[...]) * ghi_ref[0, :, sl].astype(F32)).astype(BF16)


def _fourier(cbase, sbase, crow, srow, u, cb, sb, fw, rev, g, *, nb):
    G, S, W = u.shape
    tk = rev.shape[0]
    nt = S // tk
    assert nt % 2 == 0
    full = lambda t: pl.BlockSpec(t.shape, lambda i, j: (0,) * t.ndim)
    row = pl.BlockSpec((1, 1, S), lambda i, j: (j, 0, 0))
    lo = pl.BlockSpec((1, tk, W), lambda i, j: (i, j, 0))
    hi_in = pl.BlockSpec((1, tk, W), lambda i, j: (i, nt - 1 - j, 0))
    hi_out = pl.BlockSpec((1, tk, W), lambda i, j: (i, nt // 2 - 1 - j, 0))
    half = jax.ShapeDtypeStruct((G, S // 2, W), BF16)
    return pl.pallas_call(
        functools.partial(_fourier_kernel, nb=nb), grid=(G, nt // 2),
        in_specs=[full(cbase), full(sbase), row, row, pl.BlockSpec((1, S, W), lambda i, j: (i, 0, 0)),
                  full(cb), full(sb), full(fw), full(rev), lo, hi_in],
        out_specs=(lo, hi_out), out_shape=(half, half),
        compiler_params=_params("parallel", "parallel"), name="fourier",
    )(cbase, sbase, crow, srow, u, cb, sb, fw, rev, g, g)


ATTN_PIECES = 2
RED_WAYS = 4


def _attn_kernel(q_ref, k_ref, vt_ref, g_ref, o_ref):
    S = k_ref.shape[1]
    pk = S // ATTN_PIECES

    def scores(h, j):
        sl = slice(h * HEAD_PAD, (h + 1) * HEAD_PAD)
        return lax.dot_general(k_ref[0, j * pk:(j + 1) * pk, sl], q_ref[0, :, sl], (((1,), (1,)), ((), ())),
                               preferred_element_type=F32)

    def colmax(s):
        n, tq = s.shape
        return jnp.max(jnp.max(s.reshape(RED_WAYS, n // RED_WAYS, tq), axis=1), axis=0, keepdims=True)

    def colsum(p):
        n, tq = p.shape
        return jnp.sum(jnp.sum(p.reshape(RED_WAYS, n // RED_WAYS, tq), axis=1), axis=0, keepdims=True)

    outs = []
    s_prev = [scores(0, j) for j in range(ATTN_PIECES)]
    for h in range(N_HEADS):
        m = colmax(s_prev[0])
        for j in range(1, ATTN_PIECES):
            m = jnp.maximum(m, colmax(s_prev[j]))
        s_next = []
        acc = l = None
        for j in range(ATTN_PIECES):
            if h + 1 < N_HEADS:
                s_next.append(scores(h + 1, j))
            p = jnp.exp2(s_prev[j] - m)
            lj = colsum(p)
            pv = _dot(vt_ref[0, h * V_HEAD:(h + 1) * V_HEAD, j * pk:(j + 1) * pk], p.astype(BF16))
            acc = pv if j == 0 else acc + pv
            l = lj if j == 0 else l + lj
        outs.append(acc / l)
        s_prev = s_next
    for pr in range(N_HEADS // 2):
        o = jnp.concatenate([outs[2 * pr], outs[2 * pr + 1]], axis=0).T
        sl = slice(pr * LANES, (pr + 1) * LANES)
        o_ref[0, :, sl] = (o * g_ref[0, :, sl].astype(F32)).astype(BF16)


def _attention(q, k, vt, g, *, tq):
    B, S, _ = q.shape
    qblk = pl.BlockSpec((1, tq, N_HEADS * HEAD_PAD), lambda b, i: (b, i, 0))
    oblk = pl.BlockSpec((1, tq, ATTN_WIDTH), lambda b, i: (b, i, 0))
    return pl.pallas_call(
        _attn_kernel, grid=(B, S // tq),
        in_specs=[qblk,
                  pl.BlockSpec((1, S, N_HEADS * HEAD_PAD), lambda b, i: (b, 0, 0)),
                  pl.BlockSpec((1, ATTN_WIDTH, S), lambda b, i: (b, 0, 0)),
                  oblk],
        out_specs=oblk,
        out_shape=jax.ShapeDtypeStruct((B, S, ATTN_WIDTH), BF16),
        compiler_params=_params("parallel", "parallel"), name="attention",
    )(q, k, vt, g)


def _merge_kernel(x_ref, pre_g_ref, wm_ref, gb_ref, yp_ref, ya_ref, yc_ref, yflo_ref, yfhi_ref,
                  wup_ref, wua_ref, wuc_ref, wuf_ref, wo_ref, post_g_ref, o_ref):
    x = x_ref[0]
    h = _rms(x, pre_g_ref[...]).astype(BF16)
    in_lo = pl.program_id(1) < pl.num_programs(1) // 2
    y_four = jnp.where(in_lo, yflo_ref[0], yfhi_ref[0])
    m = None
    branches = ((yp_ref[0], wup_ref), (ya_ref[0], wua_ref), (yc_ref[0], wuc_ref), (y_four, wuf_ref))
    for i, (y, wu_ref) in enumerate(branches):
        sl = slice(i * D_MODEL, (i + 1) * D_MODEL)
        gate = _sigmoid(_dot(h, wm_ref[:, sl]) + gb_ref[:, sl])
        term = gate * _dot(y, wu_ref[...])
        m = term if m is None else m + term
    out = _dot(m.astype(BF16), wo_ref[...])
    o_ref[0] = x + _rms(out, post_g_ref[...])


def _merge(x, pre_g, wm, gb, yp, ya, yc, yf_lo, yf_hi, wup, wua, wuc, wuf, wo, post_g, *, ts, nb):
    B, S, D = x.shape
    ns = S // ts
    assert ns % 2 == 0
    tok = lambda w: pl.BlockSpec((1, ts, w), lambda b, s: (b, s, 0))
    lo = pl.BlockSpec((1, ts, FOURIER_WIDTH), lambda b, s: (b // nb, jnp.minimum(s, ns // 2 - 1), b % nb))
    hi = pl.BlockSpec((1, ts, FOURIER_WIDTH), lambda b, s: (b // nb, jnp.maximum(s - ns // 2, 0), b % nb))
    full = lambda a: pl.BlockSpec(a.shape, lambda b, s: (0,) * a.ndim)
    return pl.pallas_call(
        _merge_kernel, grid=(B, ns),
        in_specs=[tok(D), full(pre_g), full(wm), full(gb), tok(POOL_WIDTH), tok(ATTN_WIDTH), tok(CONV_WIDTH),
                  lo, hi, full(wup), full(wua), full(wuc), full(wuf), full(wo), full(post_g)],
        out_specs=tok(D), out_shape=jax.ShapeDtypeStruct((B, S, D), F32),
        compiler_params=_params("parallel", "parallel"), name="merge",
    )(x, pre_g, wm, gb, yp, ya, yc, yf_lo, yf_hi, wup, wua, wuc, wuf, wo, post_g)


def _block_diag(blocks):
    g, n, _ = blocks.shape
    eye = jnp.eye(g, dtype=blocks.dtype)
    return jnp.einsum('gh,gcd->gchd', eye, blocks).reshape(g * n, g * n)


def _prep_layer(w_in, w_uq, w_ukv, pool_w):
    D = w_in.shape[0]
    offs = [0]
    for sz in (POOL_WIDTH, Q_LORA, KV_LORA, QK_ROPE, CONV_WIDTH, CONV_WIDTH, FOURIER_WIDTH, GATES_WIDTH,
               N_BRANCH * D_MODEL):
        offs.append(offs[-1] + sz)
    cols = [w_in[:, offs[i]:offs[i + 1]] for i in range(len(offs) - 1)]
    w_upool, w_cq, w_ckv, w_kr, w_ca, w_cb, w_uf, w_g, w_m = cols
    zl = jnp.zeros((D, QK_NOPE), F32)
    zr = jnp.zeros((D, HEAD_PAD - QK_DIM), F32)
    w_krp = jnp.concatenate([zl, w_kr, zr], axis=1)
    w_a = jnp.concatenate([w_upool, w_cq, w_ckv, w_krp, w_ca, w_cb, w_uf, w_g], axis=1).astype(BF16)

    wq = w_uq.reshape(Q_LORA, N_HEADS, QK_DIM)
    zpad = jnp.zeros((Q_LORA, N_HEADS, HEAD_PAD - QK_DIM), F32)
    wq_p = jnp.concatenate([wq, zpad], axis=-1).reshape(Q_LORA, N_HEADS * HEAD_PAD).astype(BF16)

    wkv = w_ukv.reshape(KV_LORA, N_HEADS, QK_NOPE + V_HEAD)
    wk = jnp.concatenate([wkv[..., :QK_NOPE], jnp.zeros((KV_LORA, N_HEADS, HEAD_PAD - QK_NOPE), F32)], axis=-1)
    wk = wk.reshape(KV_LORA, N_HEADS * HEAD_PAD).astype(BF16)
    wv = wkv[..., QK_NOPE:].reshape(KV_LORA, ATTN_WIDTH).astype(BF16)
    return w_a, w_m.astype(BF16), wq_p, wk, wv.T, _block_diag(pool_w).astype(BF16)


def _rope_tables(S):
    inv_freq = 1.0 / (ROPE_BASE ** (jnp.arange(0, QK_ROPE, 2, dtype=F32) / QK_ROPE))
    ang = jnp.arange(S, dtype=F32)[:, None] * inv_freq[None, :]
    ang = jnp.concatenate([ang, ang], axis=-1)
    half = QK_ROPE // 2
    sin = jnp.sin(ang)
    zero = lambda w: jnp.zeros((S, w), F32)
    cos_t = jnp.concatenate([jnp.ones((S, QK_NOPE), F32), jnp.cos(ang), zero(HEAD_PAD - QK_DIM)], axis=-1)
    sin_lo = jnp.concatenate([zero(QK_NOPE), -sin[:, :half], zero(HEAD_PAD - QK_NOPE - half)], axis=-1)
    sin_hi = jnp.concatenate([zero(QK_NOPE + half), sin[:, half:], zero(HEAD_PAD - QK_DIM)], axis=-1)
    return cos_t, sin_lo, sin_hi


def _dft_tables(ks, n):
    ang = ((ks[:, None] * jnp.arange(n, dtype=jnp.int32)[None, :]) % n).astype(F32) * (2.0 * math.pi / n)
    return jnp.cos(ang), jnp.sin(ang)


def _group_size(B):
    for nb in (4, 2, 1):
        if B % nb == 0:
            return nb


def _tile(S, t):
    return t if S % t == 0 else S


def _layer(x, tabs, pre_g, post_g, w_in, gate_b, q_norm_g, w_uq, kv_norm_g, w_ukv, pool_w, pool_scale,
           conv_w, conv_b, conv_ln_g, conv_ln_b, conv_pw_w, conv_pw_b, fourier_w,
           w_up_pool, w_up_attn, w_up_conv, w_up_fourier, w_out):
    B, S, D = x.shape
    nb = _group_size(B)
    ts, tq = _tile(S, 512), _tile(S, 256)
    cos_t, sin_lo, sin_hi, cbase, sbase, crow, srow, cb, sb, rev = tabs
    row = lambda a: a.reshape(1, -1)
    w_a, w_m, wq_p, wk, wvt, pool_bd = _prep_layer(w_in, w_uq, w_ukv, pool_w)

    (u_pool, q, k, vt, a_conv, u_four, g_pool, g_attn, g_conv, g_four) = _in_proj(
        x, row(pre_g), w_a, row(q_norm_g), wq_p, row(kv_norm_g), wk, wvt, cos_t, sin_lo, sin_hi, ts=ts, nb=nb)
    y_pool = _pool(u_pool, g_pool, pool_bd, row(pool_scale))
    y_conv = _conv(a_conv, g_conv, conv_w, row(conv_b), row(conv_ln_g), row(conv_ln_b),
                   conv_pw_w.astype(BF16), row(conv_pw_b))
    yf_lo, yf_hi = _fourier(cbase, sbase, crow, srow, u_four, cb, sb, fourier_w.astype(BF16), rev, g_four, nb=nb)
    y_attn = _attention(q, k, vt, g_attn, tq=tq)
    return _merge(x, row(pre_g), w_m, row(gate_b), y_pool, y_attn, y_conv, yf_lo, yf_hi,
                  w_up_pool.astype(BF16), w_up_attn.astype(BF16), w_up_conv.astype(BF16),
                  w_up_fourier.astype(BF16), w_out.astype(BF16), row(post_g), ts=ts, nb=nb)


DFT_ROWS = 256


def _tables(S):
    cos_t, sin_lo, sin_hi = _rope_tables(S)
    tk = _tile(S, DFT_ROWS)
    ext = tk + SUBLANES
    cbase, sbase = _dft_tables(jnp.arange(ext, dtype=jnp.int32), S)
    rev = (jnp.arange(tk)[:, None] + jnp.arange(ext)[None, :] == tk).astype(BF16)
    crow, srow = _dft_tables(jnp.arange(0, S, tk, dtype=jnp.int32), S)
    crow, srow = crow.reshape(S // tk, 1, S), srow.reshape(S // tk, 1, S)
    c_ch, s_ch = _dft_tables(jnp.arange(FOURIER_GROUP_DIM, dtype=jnp.int32), FOURIER_GROUP_DIM)
    norm = 1.0 / math.sqrt(S * FOURIER_GROUP_DIM)
    eye = jnp.eye(FOURIER_GROUPS, dtype=F32)
    cb = jnp.kron(eye, c_ch * norm).astype(BF16)
    sb = jnp.kron(eye, s_ch * norm).astype(BF16)
    return cos_t, sin_lo, sin_hi, cbase, sbase, crow, srow, cb, sb, rev


def _trunk(x, tabs, layer_weights):
    depth = layer_weights[0].shape[0]
    for l in range(depth):
        x = _layer(x, tabs, *[w[l] for w in layer_weights])
    return x


def kernel(x_prompt, x_sample, pre_norm_g, post_norm_g, w_in, gate_b, q_norm_g, w_uq, kv_norm_g, w_ukv, pool_w, pool_scale, conv_w, conv_b, conv_ln_g, conv_ln_b, conv_pw_w, conv_pw_b, fourier_w, w_up_pool, w_up_attn, w_up_conv, w_up_fourier, w_out):
    weights = (pre_norm_g, post_norm_g, w_in, gate_b, q_norm_g, w_uq, kv_norm_g, w_ukv, pool_w, pool_scale,
               conv_w, conv_b, conv_ln_g, conv_ln_b, conv_pw_w, conv_pw_b, fourier_w,
               w_up_pool, w_up_attn, w_up_conv, w_up_fourier, w_out)
    tabs = _tables(x_prompt.shape[1])
    assert x_sample.shape[1] == x_prompt.shape[1]
    return (_trunk(x_prompt, tabs, weights), _trunk(x_sample, tabs, weights))
```

```python
import functools
import math

import jax
import jax.numpy as jnp
from jax import lax
from jax.experimental import pallas as pl
from jax.experimental.pallas import tpu as pltpu

F32 = jnp.float32
BF16 = jnp.bfloat16

D_MODEL = 1024
POOL_WIDTH = 256
POOL_GROUPS = 4
POOL_GROUP_DIM = POOL_WIDTH // POOL_GROUPS
POOL_WINDOWS = (2, 4, 8, 16)
N_HEADS = 8
QK_NOPE = 64
QK_ROPE = 32
QK_DIM = QK_NOPE + QK_ROPE
V_HEAD = 64
Q_LORA = 256
KV_LORA = 128
ROPE_BASE = 10000.0
ATTN_WIDTH = N_HEADS * V_HEAD
CONV_WIDTH = 256
CONV_K = 31
FOURIER_WIDTH = 256
FOURIER_GROUPS = 4
FOURIER_GROUP_DIM = FOURIER_WIDTH // FOURIER_GROUPS
N_BRANCH = 4
NORM_EPS = 1e-6
LN_EPS = 1e-5

LANES = 128
SUBLANES = 8
HEAD_PAD = LANES
VMEM_LIMIT_BYTES = 56 * 1024 * 1024

Q_SCALE = (QK_DIM ** -0.5) * math.log2(math.e)

C_UPOOL = 0
C_CQ = C_UPOOL + POOL_WIDTH
C_CKV = C_CQ + Q_LORA
C_KR = C_CKV + KV_LORA
C_CONVA = C_KR + HEAD_PAD
C_CONVB = C_CONVA + CONV_WIDTH
C_UFOUR = C_CONVB + CONV_WIDTH
C_GATES = C_UFOUR + FOURIER_WIDTH
GATES_WIDTH = POOL_WIDTH + ATTN_WIDTH + CONV_WIDTH + FOURIER_WIDTH
C_END = C_GATES + GATES_WIDTH


def _params(*sem):
    return pltpu.CompilerParams(dimension_semantics=sem, vmem_limit_bytes=VMEM_LIMIT_BYTES)


def _dot(a, b):
    return jnp.dot(a, b, preferred_element_type=F32)


def _rms(x, g):
    return x * lax.rsqrt(jnp.mean(x * x, axis=-1, keepdims=True) + NORM_EPS) * g


def _sigmoid(x):
    return 1.0 / (1.0 + jnp.exp(-x))


def _rope(x, cos, sin_lo, sin_hi):
    return x * cos + pltpu.roll(x, HEAD_PAD - QK_ROPE // 2, 1) * sin_lo + pltpu.roll(x, QK_ROPE // 2, 1) * sin_hi


def _in_proj_kernel(x_ref, pre_g_ref, w_ref, qg_ref, wq_ref, kvg_ref, wk_ref, wvt_ref, cos_ref, sinlo_ref, sinhi_ref,
                    upool_ref, q_ref, k_ref, vt_ref, aconv_ref, ufour_ref,
                    gpool_ref, gattn_ref, gconv_ref, gfour_ref):
    h = _rms(x_ref[0], pre_g_ref[...]).astype(BF16)

    def proj(lo, hi):
        return _dot(h, w_ref[:, lo:hi])

    upool_ref[0] = proj(C_UPOOL, C_CQ)

    cos = cos_ref[...]
    sin_lo, sin_hi = sinlo_ref[...], sinhi_ref[...]

    cq = _rms(proj(C_CQ, C_CKV), qg_ref[...]).astype(BF16)
    q = _dot(cq, wq_ref[...])
    for hd in range(N_HEADS):
        sl = slice(hd * HEAD_PAD, (hd + 1) * HEAD_PAD)
        q_ref[0, :, sl] = (_rope(q[:, sl], cos, sin_lo, sin_hi) * Q_SCALE).astype(BF16)

    zk = proj(C_CKV, C_CONVA)
    ckv = _rms(zk[:, :KV_LORA], kvg_ref[...]).astype(BF16)
    kn = _dot(ckv, wk_ref[...])
    kr = _rope(zk[:, KV_LORA:], cos, sin_lo, sin_hi)
    for hd in range(N_HEADS):
        sl = slice(hd * HEAD_PAD, (hd + 1) * HEAD_PAD)
        k_ref[0, :, sl] = (kn[:, sl] + kr).astype(BF16)
    vt_ref[0] = lax.dot_general(wvt_ref[...], ckv, (((1,), (1,)), ((), ())),
                                preferred_element_type=F32).astype(BF16)

    aconv_ref[0] = proj(C_CONVA, C_CONVB) * _sigmoid(proj(C_CONVB, C_UFOUR))
    ufour_ref[0] = proj(C_UFOUR, C_GATES).astype(BF16)

    def gate(lo, hi):
        z = proj(C_GATES + lo, C_GATES + hi)
        return (z * _sigmoid(z)).astype(BF16)

    o1 = POOL_WIDTH
    o2 = o1 + ATTN_WIDTH
    o3 = o2 + CONV_WIDTH
    gpool_ref[0] = gate(0, o1)
    gattn_ref[0] = gate(o1, o2)
    gconv_ref[0] = gate(o2, o3)
    gfour_ref[0] = gate(o3, GATES_WIDTH)


def _in_proj(x, pre_g, w_a, qg, wq, kvg, wk, wvt, cos_t, sin_lo, sin_hi, *, ts, nb):
    B, S, D = x.shape
    grid = (B, S // ts)
    tok = lambda w: pl.BlockSpec((1, ts, w), lambda b, s: (b, s, 0))
    grp = lambda w: pl.BlockSpec((1, ts, w), lambda b, s: (b // nb, s, b % nb))
    full = lambda a: pl.BlockSpec(a.shape, lambda b, s: (0,) * a.ndim)
    rope = pl.BlockSpec((ts, HEAD_PAD), lambda b, s: (s, 0))
    sds = jax.ShapeDtypeStruct
    out_shape = (
        sds((B, S, POOL_WIDTH), F32),
        sds((B, S, N_HEADS * HEAD_PAD), BF16),
        sds((B, S, N_HEADS * HEAD_PAD), BF16),
        sds((B, ATTN_WIDTH, S), BF16),
        sds((B, S, CONV_WIDTH), F32),
        sds((B // nb, S, nb * FOURIER_WIDTH), BF16),
        sds((B, S, POOL_WIDTH), BF16),
        sds((B, S, ATTN_WIDTH), BF16),
        sds((B, S, CONV_WIDTH), BF16),
        sds((B // nb, S, nb * FOURIER_WIDTH), BF16),
    )
    vt_spec = pl.BlockSpec((1, ATTN_WIDTH, ts), lambda b, s: (b, 0, s))
    out_specs = (tok(POOL_WIDTH), tok(N_HEADS * HEAD_PAD), tok(N_HEADS * HEAD_PAD), vt_spec,
                 tok(CONV_WIDTH), grp(FOURIER_WIDTH), tok(POOL_WIDTH), tok(ATTN_WIDTH), tok(CONV_WIDTH),
                 grp(FOURIER_WIDTH))
    return pl.pallas_call(
        _in_proj_kernel, grid=grid,
        in_specs=[tok(D), full(pre_g), full(w_a), full(qg), full(wq), full(kvg), full(wk), full(wvt), rope, rope, rope],
        out_specs=out_specs, out_shape=out_shape,
        compiler_params=_params("parallel", "parallel"), name="in_proj",
    )(x, pre_g, w_a, qg, wq, kvg, wk, wvt, cos_t, sin_lo, sin_hi)


POOL_HALO = max(POOL_WINDOWS) // 2
POOL_CHUNK = 256


def _pool_kernel(u_ref, g_ref, w_ref, scale_ref, y_ref, pad_ref, *, S):
    zeros = jnp.zeros((POOL_HALO, POOL_WIDTH), F32)
    pad_ref[0:POOL_HALO, :] = zeros
    pad_ref[POOL_HALO + S:POOL_HALO + S + POOL_HALO, :] = zeros
    pad_ref[POOL_HALO:POOL_HALO + S, :] = u_ref[0]

    ch = min(POOL_CHUNK, S)
    groups_per_tile = LANES // POOL_GROUP_DIM
    first = lax.broadcasted_iota(jnp.int32, (ch, LANES), 1) < POOL_GROUP_DIM
    for c in range(S // ch):
        base = c * ch
        t = lax.broadcasted_iota(jnp.int32, (ch, LANES), 0) + base
        tiles = []
        for lt in range(POOL_WIDTH // LANES):
            lanes = slice(lt * LANES, (lt + 1) * LANES)
            h_small = POOL_WINDOWS[groups_per_tile * lt] // 2
            h_big = POOL_WINDOWS[groups_per_tile * lt + 1] // 2

            def rows(off):
                return pad_ref[POOL_HALO + base + off:POOL_HALO + base + off + ch, lanes]

            small = rows(-h_small)
            for off in range(-h_small + 1, h_small):
                small = small + rows(off)
            big = small
            for off in list(range(-h_big, -h_small)) + list(range(h_small, h_big)):
                big = big + rows(off)
            win = jnp.where(first, small, big)
            half = jnp.where(first, h_small, h_big)
            cnt = (jnp.minimum(t + half, S) - jnp.maximum(t - half, 0)).astype(F32)
            tiles.append(win / cnt - rows(0))
        p = jnp.concatenate(tiles, axis=1).astype(BF16)
        y = _dot(p, w_ref[...]) * scale_ref[...] * g_ref[0, base:base + ch, :].astype(F32)
        y_ref[0, base:base + ch, :] = y.astype(BF16)


def _pool(u, g, w_bd, scale):
    B, S, W = u.shape
    seq = pl.BlockSpec((1, S, W), lambda b: (b, 0, 0))
    full = lambda a: pl.BlockSpec(a.shape, lambda b: (0,) * a.ndim)
    return pl.pallas_call(
        functools.partial(_pool_kernel, S=S), grid=(B,),
        in_specs=[seq, seq, full(w_bd), full(scale)], out_specs=seq,
        out_shape=jax.ShapeDtypeStruct((B, S, W), BF16),
        scratch_shapes=[pltpu.VMEM((S + 2 * POOL_HALO, W), F32)],
        compiler_params=_params("parallel"), name="pool",
    )(u, g, w_bd, scale)


CONV_HALO = 16
CONV_CHUNK = 512
CONV_ROWS = 64
CONV_FIRST = CONV_HALO - CONV_K // 2
CONV_SPAN = (CONV_FIRST + CONV_K - 1) // SUBLANES * SUBLANES


def _conv_kernel(a_ref, g_ref, cw_ref, cb_ref, lng_ref, lnb_ref, pw_ref, pwb_ref, y_ref, pad_ref, sh_ref, acc_ref,
                 *, S):
    zeros = jnp.zeros((CONV_HALO, CONV_WIDTH), F32)
    pad_ref[0:CONV_HALO, :] = zeros
    pad_ref[CONV_HALO + S:CONV_HALO + S + CONV_HALO, :] = zeros
    pad_ref[CONV_HALO:CONV_HALO + S, :] = a_ref[0]

    ch = min(CONV_CHUNK, S)
    rb = min(CONV_ROWS, ch)
    for c in range(S // ch):
        base = c * ch
        for r in range(1, SUBLANES):
            sh_ref[r - 1] = pad_ref[base + r:base + r + ch + CONV_SPAN, :]
        for sb in range(ch // rb):
            row = sb * rb
            acc = jnp.zeros((rb, CONV_WIDTH), F32) + cb_ref[...]
            for k in range(CONV_K):
                off = CONV_FIRST + k
                al, r = off // SUBLANES * SUBLANES, off % SUBLANES
                if r == 0:
                    x = pad_ref[base + row + al:base + row + al + rb, :]
                else:
                    x = sh_ref[r - 1, row + al:row + al + rb, :]
                acc = acc + x * cw_ref[k:k + 1, :]
            acc_ref[row:row + rb, :] = acc
        y = acc_ref[...]
        mu = jnp.mean(y, axis=-1, keepdims=True)
        d = y - mu
        var = jnp.mean(d * d, axis=-1, keepdims=True)
        yn = d * lax.rsqrt(var + LN_EPS) * lng_ref[...] + lnb_ref[...]
        act = (yn * _sigmoid(yn)).astype(BF16)
        out = (_dot(act, pw_ref[...]) + pwb_ref[...]) * g_ref[0, base:base + ch, :].astype(F32)
        y_ref[0, base:base + ch, :] = out.astype(BF16)


def _conv(a, g, cw, cb, lng, lnb, pw, pwb):
    B, S, W = a.shape
    ch = min(CONV_CHUNK, S)
    seq = pl.BlockSpec((1, S, W), lambda b: (b, 0, 0))
    full = lambda t: pl.BlockSpec(t.shape, lambda b: (0,) * t.ndim)
    return pl.pallas_call(
        functools.partial(_conv_kernel, S=S), grid=(B,),
        in_specs=[seq, seq, full(cw), full(cb), full(lng), full(lnb), full(pw), full(pwb)], out_specs=seq,
        out_shape=jax.ShapeDtypeStruct((B, S, W), BF16),
        scratch_shapes=[pltpu.VMEM((S + 2 * CONV_HALO, W), F32),
                        pltpu.VMEM((SUBLANES - 1, ch + CONV_SPAN, W), F32),
                        pltpu.VMEM((ch, W), F32)],
        compiler_params=_params("parallel"), name="conv",
    )(a, g, cw, cb, lng, lnb, pw, pwb)


def _fourier_kernel(cbase_ref, sbase_ref, crow_ref, srow_ref, u_ref, cb_ref, sb_ref, fw_ref, rev_ref,
                    glo_ref, ghi_ref, ylo_ref, yhi_ref, *, nb):
    tk = ylo_ref.shape[1]
    c0, s0 = crow_ref[0], srow_ref[0]
    cbase, sbase = cbase_ref[...], sbase_ref[...]
    cmat = (cbase * c0 - sbase * s0).astype(BF16)
    smat = (sbase * c0 + cbase * s0).astype(BF16)
    u = u_ref[0]
    a = _dot(cmat, u).astype(BF16)
    b = _dot(smat, u).astype(BF16)
    for n in range(nb):
        sl = slice(n * FOURIER_WIDTH, (n + 1) * FOURIER_WIDTH)
        cc = _dot(a[:, sl], cb_ref[...])
        ss = _dot(b[:, sl], sb_ref[...])
        f = (cc - ss)[:tk].astype(BF16)
        fm = _dot(rev_ref[...], (cc + ss).astype(BF16)).astype(BF16)
        ylo_ref[0, :, sl] = (_dot(f, fw_ref[...]) * glo_ref[0, :, sl].astype(F32)).astype(BF16)
        yhi_ref[0, :, sl] = (_dot(fm, fw_ref[...]) * ghi_ref[0, :, sl].astype(F32)).astype(BF16)


def _fourier(cbase, sbase, crow, srow, u, cb, sb, fw, rev, g, *, nb):
    G, S, W = u.shape
    tk = rev.shape[0]
    nt = S // tk
    assert nt % 2 == 0
    full = lambda t: pl.BlockSpec(t.shape, lambda i, j: (0,) * t.ndim)
    row = pl.BlockSpec((1, 1, S), lambda i, j: (j, 0, 0))
    lo = pl.BlockSpec((1, tk, W), lambda i, j: (i, j, 0))
    hi_in = pl.BlockSpec((1, tk, W), lambda i, j: (i, nt - 1 - j, 0))
    hi_out = pl.BlockSpec((1, tk, W), lambda i, j: (i, nt // 2 - 1 - j, 0))
    half = jax.ShapeDtypeStruct((G, S // 2, W), BF16)
    return pl.pallas_call(
        functools.partial(_fourier_kernel, nb=nb), grid=(G, nt // 2),
        in_specs=[full(cbase), full(sbase), row, row, pl.BlockSpec((1, S, W), lambda i, j: (i, 0, 0)),
                  full(cb), full(sb), full(fw), full(rev), lo, hi_in],
        out_specs=(lo, hi_out), out_shape=(half, half),
        compiler_params=_params("parallel", "parallel"), name="fourier",
    )(cbase, sbase, crow, srow, u, cb, sb, fw, rev, g, g)


ATTN_PIECES = 2
RED_WAYS = 4


def _attn_kernel(q_ref, k_ref, vt_ref, g_ref, o_ref):
    S = k_ref.shape[1]
    pk = S // ATTN_PIECES

    def scores(h, j):
        sl = slice(h * HEAD_PAD, (h + 1) * HEAD_PAD)
        return lax.dot_general(k_ref[0, j * pk:(j + 1) * pk, sl], q_ref[0, :, sl], (((1,), (1,)), ((), ())),
                               preferred_element_type=F32)

    def colmax(s):
        n, tq = s.shape
        return jnp.max(jnp.max(s.reshape(RED_WAYS, n // RED_WAYS, tq), axis=1), axis=0, keepdims=True)

    def colsum(p):
        n, tq = p.shape
        return jnp.sum(jnp.sum(p.reshape(RED_WAYS, n // RED_WAYS, tq), axis=1), axis=0, keepdims=True)

    outs = []
    s_prev = [scores(0, j) for j in range(ATTN_PIECES)]
    for h in range(N_HEADS):
        m = colmax(s_prev[0])
        for j in range(1, ATTN_PIECES):
            m = jnp.maximum(m, colmax(s_prev[j]))
        s_next = []
        acc = l = None
        for j in range(ATTN_PIECES):
            if h + 1 < N_HEADS:
                s_next.append(scores(h + 1, j))
            p = jnp.exp2(s_prev[j] - m)
            lj = colsum(p)
            pv = _dot(vt_ref[0, h * V_HEAD:(h + 1) * V_HEAD, j * pk:(j + 1) * pk], p.astype(BF16))
            acc = pv if j == 0 else acc + pv
            l = lj if j == 0 else l + lj
        outs.append(acc / l)
        s_prev = s_next
    for pr in range(N_HEADS // 2):
        o = jnp.concatenate([outs[2 * pr], outs[2 * pr + 1]], axis=0).T
        sl = slice(pr * LANES, (pr + 1) * LANES)
        o_ref[0, :, sl] = (o * g_ref[0, :, sl].astype(F32)).astype(BF16)


def _attention(q, k, vt, g, *, tq):
    B, S, _ = q.shape
    qblk = pl.BlockSpec((1, tq, N_HEADS * HEAD_PAD), lambda b, i: (b, i, 0))
    oblk = pl.BlockSpec((1, tq, ATTN_WIDTH), lambda b, i: (b, i, 0))
    return pl.pallas_call(
        _attn_kernel, grid=(B, S // tq),
        in_specs=[qblk,
                  pl.BlockSpec((1, S, N_HEADS * HEAD_PAD), lambda b, i: (b, 0, 0)),
                  pl.BlockSpec((1, ATTN_WIDTH, S), lambda b, i: (b, 0, 0)),
                  oblk],
        out_specs=oblk,
        out_shape=jax.ShapeDtypeStruct((B, S, ATTN_WIDTH), BF16),
        compiler_params=_params("parallel", "parallel"), name="attention",
    )(q, k, vt, g)


def _merge_kernel(x_ref, pre_g_ref, wm_ref, gb_ref, yp_ref, ya_ref, yc_ref, yflo_ref, yfhi_ref,
                  wup_ref, wua_ref, wuc_ref, wuf_ref, wo_ref, post_g_ref, o_ref):
    x = x_ref[0]
    h = _rms(x, pre_g_ref[...]).astype(BF16)
    in_lo = pl.program_id(1) < pl.num_programs(1) // 2
    y_four = jnp.where(in_lo, yflo_ref[0], yfhi_ref[0])
    m = None
    branches = ((yp_ref[0], wup_ref), (ya_ref[0], wua_ref), (yc_ref[0], wuc_ref), (y_four, wuf_ref))
    for i, (y, wu_ref) in enumerate(branches):
        sl = slice(i * D_MODEL, (i + 1) * D_MODEL)
        gate = _sigmoid(_dot(h, wm_ref[:, sl]) + gb_ref[:, sl])
        term = gate * _dot(y, wu_ref[...])
        m = term if m is None else m + term
    out = _dot(m.astype(BF16), wo_ref[...])
    o_ref[0] = x + _rms(out, post_g_ref[...])


def _merge(x, pre_g, wm, gb, yp, ya, yc, yf_lo, yf_hi, wup, wua, wuc, wuf, wo, post_g, *, ts, nb):
    B, S, D = x.shape
    ns = S // ts
    assert ns % 2 == 0
    tok = lambda w: pl.BlockSpec((1, ts, w), lambda b, s: (b, s, 0))
    lo = pl.BlockSpec((1, ts, FOURIER_WIDTH), lambda b, s: (b // nb, jnp.minimum(s, ns // 2 - 1), b % nb))
    hi = pl.BlockSpec((1, ts, FOURIER_WIDTH), lambda b, s: (b // nb, jnp.maximum(s - ns // 2, 0), b % nb))
    full = lambda a: pl.BlockSpec(a.shape, lambda b, s: (0,) * a.ndim)
    return pl.pallas_call(
        _merge_kernel, grid=(B, ns),
        in_specs=[tok(D), full(pre_g), full(wm), full(gb), tok(POOL_WIDTH), tok(ATTN_WIDTH), tok(CONV_WIDTH),
                  lo, hi, full(wup), full(wua), full(wuc), full(wuf), full(wo), full(post_g)],
        out_specs=tok(D), out_shape=jax.ShapeDtypeStruct((B, S, D), F32),
        compiler_params=_params("parallel", "parallel"), name="merge",
    )(x, pre_g, wm, gb, yp, ya, yc, yf_lo, yf_hi, wup, wua, wuc, wuf, wo, post_g)


def _block_diag(blocks):
    g, n, _ = blocks.shape
    eye = jnp.eye(g, dtype=blocks.dtype)
    return jnp.einsum('gh,gcd->gchd', eye, blocks).reshape(g * n, g * n)


def _prep_layer(w_in, w_uq, w_ukv, pool_w):
    D = w_in.shape[0]
    offs = [0]
    for sz in (POOL_WIDTH, Q_LORA, KV_LORA, QK_ROPE, CONV_WIDTH, CONV_WIDTH, FOURIER_WIDTH, GATES_WIDTH,
               N_BRANCH * D_MODEL):
        offs.append(offs[-1] + sz)
    cols = [w_in[:, offs[i]:offs[i + 1]] for i in range(len(offs) - 1)]
    w_upool, w_cq, w_ckv, w_kr, w_ca, w_cb, w_uf, w_g, w_m = cols
    zl = jnp.zeros((D, QK_NOPE), F32)
    zr = jnp.zeros((D, HEAD_PAD - QK_DIM), F32)
    w_krp = jnp.concatenate([zl, w_kr, zr], axis=1)
    w_a = jnp.concatenate([w_upool, w_cq, w_ckv, w_krp, w_ca, w_cb, w_uf, w_g], axis=1).astype(BF16)

    wq = w_uq.reshape(Q_LORA, N_HEADS, QK_DIM)
    zpad = jnp.zeros((Q_LORA, N_HEADS, HEAD_PAD - QK_DIM), F32)
    wq_p = jnp.concatenate([wq, zpad], axis=-1).reshape(Q_LORA, N_HEADS * HEAD_PAD).astype(BF16)

    wkv = w_ukv.reshape(KV_LORA, N_HEADS, QK_NOPE + V_HEAD)
    wk = jnp.concatenate([wkv[..., :QK_NOPE], jnp.zeros((KV_LORA, N_HEADS, HEAD_PAD - QK_NOPE), F32)], axis=-1)
    wk = wk.reshape(KV_LORA, N_HEADS * HEAD_PAD).astype(BF16)
    wv = wkv[..., QK_NOPE:].reshape(KV_LORA, ATTN_WIDTH).astype(BF16)
    return w_a, w_m.astype(BF16), wq_p, wk, wv.T, _block_diag(pool_w).astype(BF16)


def _rope_tables(S):
    inv_freq = 1.0 / (ROPE_BASE ** (jnp.arange(0, QK_ROPE, 2, dtype=F32) / QK_ROPE))
    ang = jnp.arange(S, dtype=F32)[:, None] * inv_freq[None, :]
    ang = jnp.concatenate([ang, ang], axis=-1)
    half = QK_ROPE // 2
    sin = jnp.sin(ang)
    zero = lambda w: jnp.zeros((S, w), F32)
    cos_t = jnp.concatenate([jnp.ones((S, QK_NOPE), F32), jnp.cos(ang), zero(HEAD_PAD - QK_DIM)], axis=-1)
    sin_lo = jnp.concatenate([zero(QK_NOPE), -sin[:, :half], zero(HEAD_PAD - QK_NOPE - half)], axis=-1)
    sin_hi = jnp.concatenate([zero(QK_NOPE + half), sin[:, half:], zero(HEAD_PAD - QK_DIM)], axis=-1)
    return cos_t, sin_lo, sin_hi


def _dft_tables(ks, n):
    ang = ((ks[:, None] * jnp.arange(n, dtype=jnp.int32)[None, :]) % n).astype(F32) * (2.0 * math.pi / n)
    return jnp.cos(ang), jnp.sin(ang)


def _group_size(B):
    for nb in (4, 2, 1):
        if B % nb == 0:
            return nb


def _tile(S, t):
    return t if S % t == 0 else S


def _layer(x, tabs, pre_g, post_g, w_in, gate_b, q_norm_g, w_uq, kv_norm_g, w_ukv, pool_w, pool_scale,
           conv_w, conv_b, conv_ln_g, conv_ln_b, conv_pw_w, conv_pw_b, fourier_w,
           w_up_pool, w_up_attn, w_up_conv, w_up_fourier, w_out):
    B, S, D = x.shape
    nb = _group_size(B)
    ts_in, ts, tq = _tile(S, 1024), _tile(S, 512), _tile(S, 256)
    cos_t, sin_lo, sin_hi, cbase, sbase, crow, srow, cb, sb, rev = tabs
    row = lambda a: a.reshape(1, -1)
    w_a, w_m, wq_p, wk, wvt, pool_bd = _prep_layer(w_in, w_uq, w_ukv, pool_w)

    (u_pool, q, k, vt, a_conv, u_four, g_pool, g_attn, g_conv, g_four) = _in_proj(
        x, row(pre_g), w_a, row(q_norm_g), wq_p, row(kv_norm_g), wk, wvt, cos_t, sin_lo, sin_hi, ts=ts_in, nb=nb)
    y_pool = _pool(u_pool, g_pool, pool_bd, row(pool_scale))
    y_conv = _conv(a_conv, g_conv, conv_w, row(conv_b), row(conv_ln_g), row(conv_ln_b),
                   conv_pw_w.astype(BF16), row(conv_pw_b))
    yf_lo, yf_hi = _fourier(cbase, sbase, crow, srow, u_four, cb, sb, fourier_w.astype(BF16), rev, g_four, nb=nb)
    y_attn = _attention(q, k, vt, g_attn, tq=tq)
    return _merge(x, row(pre_g), w_m, row(gate_b), y_pool, y_attn, y_conv, yf_lo, yf_hi,
                  w_up_pool.astype(BF16), w_up_attn.astype(BF16), w_up_conv.astype(BF16),
                  w_up_fourier.astype(BF16), w_out.astype(BF16), row(post_g), ts=ts, nb=nb)


DFT_ROWS = 256


def _tables(S):
    cos_t, sin_lo, sin_hi = _rope_tables(S)
    tk = _tile(S, DFT_ROWS)
    ext = tk + SUBLANES
    cbase, sbase = _dft_tables(jnp.arange(ext, dtype=jnp.int32), S)
    rev = (jnp.arange(tk)[:, None] + jnp.arange(ext)[None, :] == tk).astype(BF16)
    crow, srow = _dft_tables(jnp.arange(0, S, tk, dtype=jnp.int32), S)
    crow, srow = crow.reshape(S // tk, 1, S), srow.reshape(S // tk, 1, S)
    c_ch, s_ch = _dft_tables(jnp.arange(FOURIER_GROUP_DIM, dtype=jnp.int32), FOURIER_GROUP_DIM)
    norm = 1.0 / math.sqrt(S * FOURIER_GROUP_DIM)
    eye = jnp.eye(FOURIER_GROUPS, dtype=F32)
    cb = jnp.kron(eye, c_ch * norm).astype(BF16)
    sb = jnp.kron(eye, s_ch * norm).astype(BF16)
    return cos_t, sin_lo, sin_hi, cbase, sbase, crow, srow, cb, sb, rev


def _trunk(x, tabs, layer_weights):
    depth = layer_weights[0].shape[0]
    for l in range(depth):
        x = _layer(x, tabs, *[w[l] for w in layer_weights])
    return x


def kernel(x_prompt, x_sample, pre_norm_g, post_norm_g, w_in, gate_b, q_norm_g, w_uq, kv_norm_g, w_ukv, pool_w, pool_scale, conv_w, conv_b, conv_ln_g, conv_ln_b, conv_pw_w, conv_pw_b, fourier_w, w_up_pool, w_up_attn, w_up_conv, w_up_fourier, w_out):
    weights = (pre_norm_g, post_norm_g, w_in, gate_b, q_norm_g, w_uq, kv_norm_g, w_ukv, pool_w, pool_scale,
               conv_w, conv_b, conv_ln_g, conv_ln_b, conv_pw_w, conv_pw_b, fourier_w,
               w_up_pool, w_up_attn, w_up_conv, w_up_fourier, w_out)
    tabs = _tables(x_prompt.shape[1])
    assert x_sample.shape[1] == x_prompt.shape[1]
    return (_trunk(x_prompt, tabs, weights), _trunk(x_sample, tabs, weights))
```

```python
import functools
import math

import jax
import jax.numpy as jnp
from jax import lax
from jax.experimental import pallas as pl
from jax.experimental.pallas import tpu as pltpu

F32 = jnp.float32
BF16 = jnp.bfloat16

D_MODEL = 1024
POOL_WIDTH = 256
POOL_GROUPS = 4
POOL_GROUP_DIM = POOL_WIDTH // POOL_GROUPS
POOL_WINDOWS = (2, 4, 8, 16)
N_HEADS = 8
QK_NOPE = 64
QK_ROPE = 32
QK_DIM = QK_NOPE + QK_ROPE
V_HEAD = 64
Q_LORA = 256
KV_LORA = 128
ROPE_BASE = 10000.0
ATTN_WIDTH = N_HEADS * V_HEAD
CONV_WIDTH = 256
CONV_K = 31
FOURIER_WIDTH = 256
FOURIER_GROUPS = 4
FOURIER_GROUP_DIM = FOURIER_WIDTH // FOURIER_GROUPS
N_BRANCH = 4
NORM_EPS = 1e-6
LN_EPS = 1e-5

LANES = 128
SUBLANES = 8
HEAD_PAD = LANES
VMEM_LIMIT_BYTES = 56 * 1024 * 1024

Q_SCALE = (QK_DIM ** -0.5) * math.log2(math.e)

C_UPOOL = 0
C_CQ = C_UPOOL + POOL_WIDTH
C_CKV = C_CQ + Q_LORA
C_KR = C_CKV + KV_LORA
C_CONVA = C_KR + HEAD_PAD
C_CONVB = C_CONVA + CONV_WIDTH
C_UFOUR = C_CONVB + CONV_WIDTH
C_GATES = C_UFOUR + FOURIER_WIDTH
GATES_WIDTH = POOL_WIDTH + ATTN_WIDTH + CONV_WIDTH + FOURIER_WIDTH
C_END = C_GATES + GATES_WIDTH


def _params(*sem):
    return pltpu.CompilerParams(dimension_semantics=sem, vmem_limit_bytes=VMEM_LIMIT_BYTES)


def _dot(a, b):
    return jnp.dot(a, b, preferred_element_type=F32)


def _rms(x, g):
    return x * lax.rsqrt(jnp.mean(x * x, axis=-1, keepdims=True) + NORM_EPS) * g


def _sigmoid(x):
    return 1.0 / (1.0 + jnp.exp(-x))


def _rope(x, cos, sin_lo, sin_hi):
    return x * cos + pltpu.roll(x, HEAD_PAD - QK_ROPE // 2, 1) * sin_lo + pltpu.roll(x, QK_ROPE // 2, 1) * sin_hi


def _in_proj_kernel(x_ref, pre_g_ref, w_ref, qg_ref, wq_ref, kvg_ref, wk_ref, wvt_ref, cos_ref, sinlo_ref, sinhi_ref,
                    upool_ref, q_ref, k_ref, vt_ref, aconv_ref, ufour_ref,
                    gpool_ref, gattn_ref, gconv_ref, gfour_ref):
    h = _rms(x_ref[0], pre_g_ref[...]).astype(BF16)

    def proj(lo, hi):
        return _dot(h, w_ref[:, lo:hi])

    upool_ref[0] = proj(C_UPOOL, C_CQ)

    cos = cos_ref[...]
    sin_lo, sin_hi = sinlo_ref[...], sinhi_ref[...]

    cq = _rms(proj(C_CQ, C_CKV), qg_ref[...]).astype(BF16)
    q = _dot(cq, wq_ref[...])
    for hd in range(N_HEADS):
        sl = slice(hd * HEAD_PAD, (hd + 1) * HEAD_PAD)
        q_ref[0, :, sl] = (_rope(q[:, sl], cos, sin_lo, sin_hi) * Q_SCALE).astype(BF16)

    zk = proj(C_CKV, C_CONVA)
    ckv = _rms(zk[:, :KV_LORA], kvg_ref[...]).astype(BF16)
    kn = _dot(ckv, wk_ref[...])
    kr = _rope(zk[:, KV_LORA:], cos, sin_lo, sin_hi)
    for hd in range(N_HEADS):
        sl = slice(hd * HEAD_PAD, (hd + 1) * HEAD_PAD)
        k_ref[0, :, sl] = (kn[:, sl] + kr).astype(BF16)
    vt_ref[0] = lax.dot_general(wvt_ref[...], ckv, (((1,), (1,)), ((), ())),
                                preferred_element_type=F32).astype(BF16)

    aconv_ref[0] = proj(C_CONVA, C_CONVB) * _sigmoid(proj(C_CONVB, C_UFOUR))
    ufour_ref[0] = proj(C_UFOUR, C_GATES).astype(BF16)

    def gate(lo, hi):
        z = proj(C_GATES + lo, C_GATES + hi)
        return (z * _sigmoid(z)).astype(BF16)

    o1 = POOL_WIDTH
    o2 = o1 + ATTN_WIDTH
    o3 = o2 + CONV_WIDTH
    gpool_ref[0] = gate(0, o1)
    gattn_ref[0] = gate(o1, o2)
    gconv_ref[0] = gate(o2, o3)
    gfour_ref[0] = gate(o3, GATES_WIDTH)


def _in_proj(x, pre_g, w_a, qg, wq, kvg, wk, wvt, cos_t, sin_lo, sin_hi, *, ts, nb):
    B, S, D = x.shape
    grid = (B, S // ts)
    tok = lambda w: pl.BlockSpec((1, ts, w), lambda b, s: (b, s, 0))
    grp = lambda w: pl.BlockSpec((1, ts, w), lambda b, s: (b // nb, s, b % nb))
    full = lambda a: pl.BlockSpec(a.shape, lambda b, s: (0,) * a.ndim)
    rope = pl.BlockSpec((ts, HEAD_PAD), lambda b, s: (s, 0))
    sds = jax.ShapeDtypeStruct
    out_shape = (
        sds((B, S, POOL_WIDTH), F32),
        sds((B, S, N_HEADS * HEAD_PAD), BF16),
        sds((B, S, N_HEADS * HEAD_PAD), BF16),
        sds((B, ATTN_WIDTH, S), BF16),
        sds((B, S, CONV_WIDTH), F32),
        sds((B // nb, S, nb * FOURIER_WIDTH), BF16),
        sds((B, S, POOL_WIDTH), BF16),
        sds((B, S, ATTN_WIDTH), BF16),
        sds((B, S, CONV_WIDTH), BF16),
        sds((B // nb, S, nb * FOURIER_WIDTH), BF16),
    )
    vt_spec = pl.BlockSpec((1, ATTN_WIDTH, ts), lambda b, s: (b, 0, s))
    out_specs = (tok(POOL_WIDTH), tok(N_HEADS * HEAD_PAD), tok(N_HEADS * HEAD_PAD), vt_spec,
                 tok(CONV_WIDTH), grp(FOURIER_WIDTH), tok(POOL_WIDTH), tok(ATTN_WIDTH), tok(CONV_WIDTH),
                 grp(FOURIER_WIDTH))
    return pl.pallas_call(
        _in_proj_kernel, grid=grid,
        in_specs=[tok(D), full(pre_g), full(w_a), full(qg), full(wq), full(kvg), full(wk), full(wvt), rope, rope, rope],
        out_specs=out_specs, out_shape=out_shape,
        compiler_params=_params("parallel", "parallel"), name="in_proj",
    )(x, pre_g, w_a, qg, wq, kvg, wk, wvt, cos_t, sin_lo, sin_hi)


POOL_HALO = max(POOL_WINDOWS) // 2
POOL_CHUNK = 256


def _pool_kernel(u_ref, g_ref, w_ref, scale_ref, y_ref, pad_ref, *, S):
    zeros = jnp.zeros((POOL_HALO, POOL_WIDTH), F32)
    pad_ref[0:POOL_HALO, :] = zeros
    pad_ref[POOL_HALO + S:POOL_HALO + S + POOL_HALO, :] = zeros
    pad_ref[POOL_HALO:POOL_HALO + S, :] = u_ref[0]

    ch = min(POOL_CHUNK, S)
    groups_per_tile = LANES // POOL_GROUP_DIM
    first = lax.broadcasted_iota(jnp.int32, (ch, LANES), 1) < POOL_GROUP_DIM
    for c in range(S // ch):
        base = c * ch
        t = lax.broadcasted_iota(jnp.int32, (ch, LANES), 0) + base
        tiles = []
        for lt in range(POOL_WIDTH // LANES):
            lanes = slice(lt * LANES, (lt + 1) * LANES)
            h_small = POOL_WINDOWS[groups_per_tile * lt] // 2
            h_big = POOL_WINDOWS[groups_per_tile * lt + 1] // 2

            def rows(off):
                return pad_ref[POOL_HALO + base + off:POOL_HALO + base + off + ch, lanes]

            small = rows(-h_small)
            for off in range(-h_small + 1, h_small):
                small = small + rows(off)
            big = small
            for off in list(range(-h_big, -h_small)) + list(range(h_small, h_big)):
                big = big + rows(off)
            win = jnp.where(first, small, big)
            half = jnp.where(first, h_small, h_big)
            cnt = (jnp.minimum(t + half, S) - jnp.maximum(t - half, 0)).astype(F32)
            tiles.append(win / cnt - rows(0))
        p = jnp.concatenate(tiles, axis=1).astype(BF16)
        y = _dot(p, w_ref[...]) * scale_ref[...] * g_ref[0, base:base + ch, :].astype(F32)
        y_ref[0, base:base + ch, :] = y.astype(BF16)


def _pool(u, g, w_bd, scale):
    B, S, W = u.shape
    seq = pl.BlockSpec((1, S, W), lambda b: (b, 0, 0))
    full = lambda a: pl.BlockSpec(a.shape, lambda b: (0,) * a.ndim)
    return pl.pallas_call(
        functools.partial(_pool_kernel, S=S), grid=(B,),
        in_specs=[seq, seq, full(w_bd), full(scale)], out_specs=seq,
        out_shape=jax.ShapeDtypeStruct((B, S, W), BF16),
        scratch_shapes=[pltpu.VMEM((S + 2 * POOL_HALO, W), F32)],
        compiler_params=_params("parallel"), name="pool",
    )(u, g, w_bd, scale)


CONV_HALO = 16
CONV_CHUNK = 512
CONV_ROWS = 64
CONV_FIRST = CONV_HALO - CONV_K // 2
CONV_SPAN = (CONV_FIRST + CONV_K - 1) // SUBLANES * SUBLANES


def _conv_kernel(a_ref, g_ref, cw_ref, cb_ref, lng_ref, lnb_ref, pw_ref, pwb_ref, y_ref, pad_ref, sh_ref, acc_ref,
                 *, S):
    zeros = jnp.zeros((CONV_HALO, CONV_WIDTH), F32)
    pad_ref[0:CONV_HALO, :] = zeros
    pad_ref[CONV_HALO + S:CONV_HALO + S + CONV_HALO, :] = zeros
    pad_ref[CONV_HALO:CONV_HALO + S, :] = a_ref[0]

    ch = min(CONV_CHUNK, S)
    rb = min(CONV_ROWS, ch)
    for c in range(S // ch):
        base = c * ch
        for r in range(1, SUBLANES):
            sh_ref[r - 1] = pad_ref[base + r:base + r + ch + CONV_SPAN, :]
        for sb in range(ch // rb):
            row = sb * rb
            acc = jnp.zeros((rb, CONV_WIDTH), F32) + cb_ref[...]
            for k in range(CONV_K):
                off = CONV_FIRST + k
                al, r = off // SUBLANES * SUBLANES, off % SUBLANES
                if r == 0:
                    x = pad_ref[base + row + al:base + row + al + rb, :]
                else:
                    x = sh_ref[r - 1, row + al:row + al + rb, :]
                acc = acc + x * cw_ref[k:k + 1, :]
            acc_ref[row:row + rb, :] = acc
        y = acc_ref[...]
        mu = jnp.mean(y, axis=-1, keepdims=True)
        d = y - mu
        var = jnp.mean(d * d, axis=-1, keepdims=True)
        yn = d * lax.rsqrt(var + LN_EPS) * lng_ref[...] + lnb_ref[...]
        act = (yn * _sigmoid(yn)).astype(BF16)
        out = (_dot(act, pw_ref[...]) + pwb_ref[...]) * g_ref[0, base:base + ch, :].astype(F32)
        y_ref[0, base:base + ch, :] = out.astype(BF16)


def _conv(a, g, cw, cb, lng, lnb, pw, pwb):
    B, S, W = a.shape
    ch = min(CONV_CHUNK, S)
    seq = pl.BlockSpec((1, S, W), lambda b: (b, 0, 0))
    full = lambda t: pl.BlockSpec(t.shape, lambda b: (0,) * t.ndim)
    return pl.pallas_call(
        functools.partial(_conv_kernel, S=S), grid=(B,),
        in_specs=[seq, seq, full(cw), full(cb), full(lng), full(lnb), full(pw), full(pwb)], out_specs=seq,
        out_shape=jax.ShapeDtypeStruct((B, S, W), BF16),
        scratch_shapes=[pltpu.VMEM((S + 2 * CONV_HALO, W), F32),
                        pltpu.VMEM((SUBLANES - 1, ch + CONV_SPAN, W), F32),
                        pltpu.VMEM((ch, W), F32)],
        compiler_params=_params("parallel"), name="conv",
    )(a, g, cw, cb, lng, lnb, pw, pwb)


def _fourier_kernel(cbase_ref, sbase_ref, crow_ref, srow_ref, u_ref, cb_ref, sb_ref, fw_ref, rev_ref,
                    glo_ref, ghi_ref, ylo_ref, yhi_ref, *, nb):
    tk = ylo_ref.shape[1]
    c0, s0 = crow_ref[0], srow_ref[0]
    cbase, sbase = cbase_ref[...], sbase_ref[...]
    cmat = (cbase * c0 - sbase * s0).astype(BF16)
    smat = (sbase * c0 + cbase * s0).astype(BF16)
    u = u_ref[0]
    a = _dot(cmat, u).astype(BF16)
    b = _dot(smat, u).astype(BF16)
    for n in range(nb):
        sl = slice(n * FOURIER_WIDTH, (n + 1) * FOURIER_WIDTH)
        cc = _dot(a[:, sl], cb_ref[...])
        ss = _dot(b[:, sl], sb_ref[...])
        f = (cc - ss)[:tk].astype(BF16)
        fm = _dot(rev_ref[...], (cc + ss).astype(BF16)).astype(BF16)
        ylo_ref[0, :, sl] = (_dot(f, fw_ref[...]) * glo_ref[0, :, sl].astype(F32)).astype(BF16)
        yhi_ref[0, :, sl] = (_dot(fm, fw_ref[...]) * ghi_ref[0, :, sl].astype(F32)).astype(BF16)


def _fourier(cbase, sbase, crow, srow, u, cb, sb, fw, rev, g, *, nb):
    G, S, W = u.shape
    tk = rev.shape[0]
    nt = S // tk
    assert nt % 2 == 0
    full = lambda t: pl.BlockSpec(t.shape, lambda i, j: (0,) * t.ndim)
    row = pl.BlockSpec((1, 1, S), lambda i, j: (j, 0, 0))
    lo = pl.BlockSpec((1, tk, W), lambda i, j: (i, j, 0))
    hi_in = pl.BlockSpec((1, tk, W), lambda i, j: (i, nt - 1 - j, 0))
    hi_out = pl.BlockSpec((1, tk, W), lambda i, j: (i, nt // 2 - 1 - j, 0))
    half = jax.ShapeDtypeStruct((G, S // 2, W), BF16)
    return pl.pallas_call(
        functools.partial(_fourier_kernel, nb=nb), grid=(G, nt // 2),
        in_specs=[full(cbase), full(sbase), row, row, pl.BlockSpec((1, S, W), lambda i, j: (i, 0, 0)),
                  full(cb), full(sb), full(fw), full(rev), lo, hi_in],
        out_specs=(lo, hi_out), out_shape=(half, half),
        compiler_params=_params("parallel", "parallel"), name="fourier",
    )(cbase, sbase, crow, srow, u, cb, sb, fw, rev, g, g)


ATTN_PIECES = 2
RED_WAYS = 4


def _attn_kernel(q_ref, k_ref, vt_ref, g_ref, o_ref):
    S = k_ref.shape[1]
    pk = S // ATTN_PIECES

    def scores(h, j):
        sl = slice(h * HEAD_PAD, (h + 1) * HEAD_PAD)
        return lax.dot_general(k_ref[0, j * pk:(j + 1) * pk, sl], q_ref[0, :, sl], (((1,), (1,)), ((), ())),
                               preferred_element_type=F32)

    def colmax(s):
        n, tq = s.shape
        return jnp.max(jnp.max(s.reshape(RED_WAYS, n // RED_WAYS, tq), axis=1), axis=0, keepdims=True)

    def colsum(p):
        n, tq = p.shape
        return jnp.sum(jnp.sum(p.reshape(RED_WAYS, n // RED_WAYS, tq), axis=1), axis=0, keepdims=True)

    outs = []
    s_prev = [scores(0, j) for j in range(ATTN_PIECES)]
    for h in range(N_HEADS):
        m = colmax(s_prev[0])
        for j in range(1, ATTN_PIECES):
            m = jnp.maximum(m, colmax(s_prev[j]))
        s_next = []
        acc = l = None
        for j in range(ATTN_PIECES):
            if h + 1 < N_HEADS:
                s_next.append(scores(h + 1, j))
            p = jnp.exp2(s_prev[j] - m)
            lj = colsum(p)
            pv = _dot(vt_ref[0, h * V_HEAD:(h + 1) * V_HEAD, j * pk:(j + 1) * pk], p.astype(BF16))
            acc = pv if j == 0 else acc + pv
            l = lj if j == 0 else l + lj
        outs.append(acc / l)
        s_prev = s_next
    for pr in range(N_HEADS // 2):
        o = jnp.concatenate([outs[2 * pr], outs[2 * pr + 1]], axis=0).T
        sl = slice(pr * LANES, (pr + 1) * LANES)
        o_ref[0, :, sl] = (o * g_ref[0, :, sl].astype(F32)).astype(BF16)


def _attention(q, k, vt, g, *, tq):
    B, S, _ = q.shape
    qblk = pl.BlockSpec((1, tq, N_HEADS * HEAD_PAD), lambda b, i: (b, i, 0))
    oblk = pl.BlockSpec((1, tq, ATTN_WIDTH), lambda b, i: (b, i, 0))
    return pl.pallas_call(
        _attn_kernel, grid=(B, S // tq),
        in_specs=[qblk,
                  pl.BlockSpec((1, S, N_HEADS * HEAD_PAD), lambda b, i: (b, 0, 0)),
                  pl.BlockSpec((1, ATTN_WIDTH, S), lambda b, i: (b, 0, 0)),
                  oblk],
        out_specs=oblk,
        out_shape=jax.ShapeDtypeStruct((B, S, ATTN_WIDTH), BF16),
        compiler_params=_params("parallel", "parallel"), name="attention",
    )(q, k, vt, g)


def _merge_kernel(x_ref, pre_g_ref, wm_ref, gb_ref, yp_ref, ya_ref, yc_ref, yflo_ref, yfhi_ref,
                  wup_ref, wua_ref, wuc_ref, wuf_ref, wo_ref, post_g_ref, o_ref):
    x = x_ref[0]
    h = _rms(x, pre_g_ref[...]).astype(BF16)
    in_lo = pl.program_id(1) < pl.num_programs(1) // 2
    y_four = jnp.where(in_lo, yflo_ref[0], yfhi_ref[0])
    m = None
    branches = ((yp_ref[0], wup_ref), (ya_ref[0], wua_ref), (yc_ref[0], wuc_ref), (y_four, wuf_ref))
    for i, (y, wu_ref) in enumerate(branches):
        sl = slice(i * D_MODEL, (i + 1) * D_MODEL)
        gate = _sigmoid(_dot(h, wm_ref[:, sl]) + gb_ref[:, sl])
        term = gate * _dot(y, wu_ref[...])
        m = term if m is None else m + term
    out = _dot(m.astype(BF16), wo_ref[...])
    o_ref[0] = x + _rms(out, post_g_ref[...])


def _merge(x, pre_g, wm, gb, yp, ya, yc, yf_lo, yf_hi, wup, wua, wuc, wuf, wo, post_g, *, ts, nb):
    B, S, D = x.shape
    ns = S // ts
    assert ns % 2 == 0
    tok = lambda w: pl.BlockSpec((1, ts, w), lambda b, s: (b, s, 0))
    lo = pl.BlockSpec((1, ts, FOURIER_WIDTH), lambda b, s: (b // nb, jnp.minimum(s, ns // 2 - 1), b % nb))
    hi = pl.BlockSpec((1, ts, FOURIER_WIDTH), lambda b, s: (b // nb, jnp.maximum(s - ns // 2, 0), b % nb))
    full = lambda a: pl.BlockSpec(a.shape, lambda b, s: (0,) * a.ndim)
    return pl.pallas_call(
        _merge_kernel, grid=(B, ns),
        in_specs=[tok(D), full(pre_g), full(wm), full(gb), tok(POOL_WIDTH), tok(ATTN_WIDTH), tok(CONV_WIDTH),
                  lo, hi, full(wup), full(wua), full(wuc), full(wuf), full(wo), full(post_g)],
        out_specs=tok(D), out_shape=jax.ShapeDtypeStruct((B, S, D), F32),
        compiler_params=_params("parallel", "parallel"), name="merge",
    )(x, pre_g, wm, gb, yp, ya, yc, yf_lo, yf_hi, wup, wua, wuc, wuf, wo, post_g)


def _block_diag(blocks):
    g, n, _ = blocks.shape
    eye = jnp.eye(g, dtype=blocks.dtype)
    return jnp.einsum('gh,gcd->gchd', eye, blocks).reshape(g * n, g * n)


def _prep_layer(w_in, w_uq, w_ukv, pool_w):
    D = w_in.shape[0]
    offs = [0]
    for sz in (POOL_WIDTH, Q_LORA, KV_LORA, QK_ROPE, CONV_WIDTH, CONV_WIDTH, FOURIER_WIDTH, GATES_WIDTH,
               N_BRANCH * D_MODEL):
        offs.append(offs[-1] + sz)
    cols = [w_in[:, offs[i]:offs[i + 1]] for i in range(len(offs) - 1)]
    w_upool, w_cq, w_ckv, w_kr, w_ca, w_cb, w_uf, w_g, w_m = cols
    zl = jnp.zeros((D, QK_NOPE), F32)
    zr = jnp.zeros((D, HEAD_PAD - QK_DIM), F32)
    w_krp = jnp.concatenate([zl, w_kr, zr], axis=1)
    w_a = jnp.concatenate([w_upool, w_cq, w_ckv, w_krp, w_ca, w_cb, w_uf, w_g], axis=1).astype(BF16)

    wq = w_uq.reshape(Q_LORA, N_HEADS, QK_DIM)
    zpad = jnp.zeros((Q_LORA, N_HEADS, HEAD_PAD - QK_DIM), F32)
    wq_p = jnp.concatenate([wq, zpad], axis=-1).reshape(Q_LORA, N_HEADS * HEAD_PAD).astype(BF16)

    wkv = w_ukv.reshape(KV_LORA, N_HEADS, QK_NOPE + V_HEAD)
    wk = jnp.concatenate([wkv[..., :QK_NOPE], jnp.zeros((KV_LORA, N_HEADS, HEAD_PAD - QK_NOPE), F32)], axis=-1)
    wk = wk.reshape(KV_LORA, N_HEADS * HEAD_PAD).astype(BF16)
    wv = wkv[..., QK_NOPE:].reshape(KV_LORA, ATTN_WIDTH).astype(BF16)
    return w_a, w_m.astype(BF16), wq_p, wk, wv.T, _block_diag(pool_w).astype(BF16)


def _rope_tables(S):
    inv_freq = 1.0 / (ROPE_BASE ** (jnp.arange(0, QK_ROPE, 2, dtype=F32) / QK_ROPE))
    ang = jnp.arange(S, dtype=F32)[:, None] * inv_freq[None, :]
    ang = jnp.concatenate([ang, ang], axis=-1)
    half = QK_ROPE // 2
    sin = jnp.sin(ang)
    zero = lambda w: jnp.zeros((S, w), F32)
    cos_t = jnp.concatenate([jnp.ones((S, QK_NOPE), F32), jnp.cos(ang), zero(HEAD_PAD - QK_DIM)], axis=-1)
    sin_lo = jnp.concatenate([zero(QK_NOPE), -sin[:, :half], zero(HEAD_PAD - QK_NOPE - half)], axis=-1)
    sin_hi = jnp.concatenate([zero(QK_NOPE + half), sin[:, half:], zero(HEAD_PAD - QK_DIM)], axis=-1)
    return cos_t, sin_lo, sin_hi


def _dft_tables(ks, n):
    ang = ((ks[:, None] * jnp.arange(n, dtype=jnp.int32)[None, :]) % n).astype(F32) * (2.0 * math.pi / n)
    return jnp.cos(ang), jnp.sin(ang)


def _group_size(B):
    for nb in (4, 2, 1):
        if B % nb == 0:
            return nb


def _tile(S, t):
    return t if S % t == 0 else S


def _layer(x, tabs, pre_g, post_g, w_in, gate_b, q_norm_g, w_uq, kv_norm_g, w_ukv, pool_w, pool_scale,
           conv_w, conv_b, conv_ln_g, conv_ln_b, conv_pw_w, conv_pw_b, fourier_w,
           w_up_pool, w_up_attn, w_up_conv, w_up_fourier, w_out):
    B, S, D = x.shape
    nb = _group_size(B)
    ts_in, ts, tq = _tile(S, 1024), _tile(S, 1024), _tile(S, 256)
    cos_t, sin_lo, sin_hi, cbase, sbase, crow, srow, cb, sb, rev = tabs
    row = lambda a: a.reshape(1, -1)
    w_a, w_m, wq_p, wk, wvt, pool_bd = _prep_layer(w_in, w_uq, w_ukv, pool_w)

    (u_pool, q, k, vt, a_conv, u_four, g_pool, g_attn, g_conv, g_four) = _in_proj(
        x, row(pre_g), w_a, row(q_norm_g), wq_p, row(kv_norm_g), wk, wvt, cos_t, sin_lo, sin_hi, ts=ts_in, nb=nb)
    y_pool = _pool(u_pool, g_pool, pool_bd, row(pool_scale))
    y_conv = _conv(a_conv, g_conv, conv_w, row(conv_b), row(conv_ln_g), row(conv_ln_b),
                   conv_pw_w.astype(BF16), row(conv_pw_b))
    yf_lo, yf_hi = _fourier(cbase, sbase, crow, srow, u_four, cb, sb, fourier_w.astype(BF16), rev, g_four, nb=nb)
    y_attn = _attention(q, k, vt, g_attn, tq=tq)
    return _merge(x, row(pre_g), w_m, row(gate_b), y_pool, y_attn, y_conv, yf_lo, yf_hi,
                  w_up_pool.astype(BF16), w_up_attn.astype(BF16), w_up_conv.astype(BF16),
                  w_up_fourier.astype(BF16), w_out.astype(BF16), row(post_g), ts=ts, nb=nb)


DFT_ROWS = 256


def _tables(S):
    cos_t, sin_lo, sin_hi = _rope_tables(S)
    tk = _tile(S, DFT_ROWS)
    ext = tk + SUBLANES
    cbase, sbase = _dft_tables(jnp.arange(ext, dtype=jnp.int32), S)
    rev = (jnp.arange(tk)[:, None] + jnp.arange(ext)[None, :] == tk).astype(BF16)
    crow, srow = _dft_tables(jnp.arange(0, S, tk, dtype=jnp.int32), S)
    crow, srow = crow.reshape(S // tk, 1, S), srow.reshape(S // tk, 1, S)
    c_ch, s_ch = _dft_tables(jnp.arange(FOURIER_GROUP_DIM, dtype=jnp.int32), FOURIER_GROUP_DIM)
    norm = 1.0 / math.sqrt(S * FOURIER_GROUP_DIM)
    eye = jnp.eye(FOURIER_GROUPS, dtype=F32)
    cb = jnp.kron(eye, c_ch * norm).astype(BF16)
    sb = jnp.kron(eye, s_ch * norm).astype(BF16)
    return cos_t, sin_lo, sin_hi, cbase, sbase, crow, srow, cb, sb, rev


def _trunk(x, tabs, layer_weights):
    depth = layer_weights[0].shape[0]
    for l in range(depth):
        x = _layer(x, tabs, *[w[l] for w in layer_weights])
    return x


def kernel(x_prompt, x_sample, pre_norm_g, post_norm_g, w_in, gate_b, q_norm_g, w_uq, kv_norm_g, w_ukv, pool_w, pool_scale, conv_w, conv_b, conv_ln_g, conv_ln_b, conv_pw_w, conv_pw_b, fourier_w, w_up_pool, w_up_attn, w_up_conv, w_up_fourier, w_out):
    weights = (pre_norm_g, post_norm_g, w_in, gate_b, q_norm_g, w_uq, kv_norm_g, w_ukv, pool_w, pool_scale,
               conv_w, conv_b, conv_ln_g, conv_ln_b, conv_pw_w, conv_pw_b, fourier_w,
               w_up_pool, w_up_attn, w_up_conv, w_up_fourier, w_out)
    tabs = _tables(x_prompt.shape[1])
    assert x_sample.shape[1] == x_prompt.shape[1]
    return (_trunk(x_prompt, tabs, weights), _trunk(x_sample, tabs, weights))
```
